```python
import math
import jax, jax.numpy as jnp
from jax import lax
import numpy as np

D_MODEL = 1024
BATCH = 8
SEQ = 2048
DEPTH = 2
DEC_BATCH = 128
DEC_SEQ = 4
PAST_LEN = 8192
PAGE_SIZE = 128

H_A = 8
DN = 64
DR = 32
DV_A = 64
KV_LORA = 256
MLA_SCALE = (DN + DR) ** -0.5
ROPE_THETA = 10000.0
H_B = 8
DH = 32
DV_B = 2 * DH
DIFF_SCALE = DH ** -0.5
N_BUCKETS = 32
MAX_DIST = 128
D_FF = 4 * D_MODEL
D_PLE = 256
ALPHA = (2 * DEPTH) ** 0.25
BETA = (8 * DEPTH) ** -0.25
EPS = 1e-5
Q_BLOCK = 128
NEG = -1e30
SPLIT_SIZES = (H_A * DN, H_A * DR, KV_LORA, DR, H_B * 2 * DH, H_B * 2 * DH, H_B * DV_B, D_MODEL, D_MODEL)
D_IN = H_A * DN + H_A * DR + KV_LORA + DR + 3 * H_B * 2 * DH + 2 * D_MODEL

kernel_name = 'mla_diffattn_gated_deepnorm_step'


def layer_norm(x, g, b):
    xf = x.astype(jnp.float32)
    xc = xf - jnp.mean(xf, -1, keepdims=True)
    var = jnp.mean(xc * xc, -1, keepdims=True)
    return (xc * lax.rsqrt(var + EPS) * g.astype(jnp.float32) + b.astype(jnp.float32)).astype(x.dtype)


def rms_norm(x, g):
    xf = x.astype(jnp.float32)
    return (xf * lax.rsqrt(jnp.mean(xf * xf, -1, keepdims=True) + EPS) * g.astype(jnp.float32)).astype(x.dtype)


def rope(x, pos):
    half = DR // 2
    inv = ROPE_THETA ** (-jnp.arange(half, dtype=jnp.float32) / half)
    ang = pos.astype(jnp.float32)[:, None] * inv[None, :]
    cos = jnp.cos(ang)[:, None, :]
    sin = jnp.sin(ang)[:, None, :]
    x1 = x[..., :half].astype(jnp.float32)
    x2 = x[..., half:].astype(jnp.float32)
    return jnp.concatenate([x1 * cos - x2 * sin, x2 * cos + x1 * sin], -1).astype(x.dtype)


def t5_bucket(q_pos, k_pos):
    n = jnp.maximum(q_pos[:, None] - k_pos[None, :], 0)
    exact = N_BUCKETS // 2
    nf = jnp.maximum(n, 1).astype(jnp.float32)
    large = exact + (jnp.log(nf / exact) / math.log(MAX_DIST / exact) * (N_BUCKETS - exact)).astype(jnp.int32)
    large = jnp.minimum(large, N_BUCKETS - 1)
    return jnp.where(n < exact, n, large)


def rel_bias(table, q_pos, k_pos):
    return jnp.transpose(table[t5_bucket(q_pos, k_pos)], (2, 0, 1)).astype(jnp.float32)


def mla_core(q_nope, q_rope, c_kv, k_rope, mask, w_uk, w_uv):
    q_lat = jnp.einsum('bqhd,chd->bqhc', q_nope, w_uk)
    s = jnp.einsum('bqhc,bkc->bhqk', q_lat, c_kv) + jnp.einsum('bqhr,bkr->bhqk', q_rope, k_rope)
    s = jnp.where(mask, s.astype(jnp.float32) * MLA_SCALE, NEG)
    p = jax.nn.softmax(s, axis=-1).astype(c_kv.dtype)
    o_lat = jnp.einsum('bhqk,bkc->bqhc', p, c_kv)
    o = jnp.einsum('bqhc,chv->bqhv', o_lat, w_uv)
    return o.reshape(o.shape[0], o.shape[1], H_A * DV_A)


def diff_core(q, k, v, bias, mask, lam, lam_init, subln_g):
    s = jnp.einsum('bqhjd,bkhjd->bjhqk', q, k).astype(jnp.float32) * DIFF_SCALE + bias
    s = jnp.where(mask, s, NEG)
    a = jax.nn.softmax(s, axis=-1)
    attn = (a[:, 0] - lam * a[:, 1]).astype(v.dtype)
    o = jnp.einsum('bhqk,bkhv->bqhv', attn, v)
    o = rms_norm(o, subln_g) * (1.0 - lam_init)
    return o.reshape(o.shape[0], o.shape[1], H_B * DV_B)


def mixer_inputs(x, pos, w_in, kv_g):
    b, s, _ = x.shape
    z = x @ w_in
    cuts = np.cumsum(SPLIT_SIZES)[:-1].tolist()
    qn, qr, ckv, kr, dq, dk, dv, ga, gb = jnp.split(z, cuts, axis=-1)
    return (qn.reshape(b, s, H_A, DN),
            rope(qr.reshape(b, s, H_A, DR), pos),
            rms_norm(ckv, kv_g),
            rope(kr[:, :, None, :], pos)[:, :, 0],
            dq.reshape(b, s, H_B, 2, DH),
            dk.reshape(b, s, H_B, 2, DH),
            dv.reshape(b, s, H_B, DV_B),
            jax.nn.sigmoid(ga),
            jax.nn.sigmoid(gb))


def merge_and_channel(x, o_a, o_b, g_a, g_b, p, w_a, w_b, w_out, ln1_g, ln1_b, w_up, w_down, ln2_g, ln2_b, w_p, w_pg):
    mix = (g_a * (o_a @ w_a) + g_b * (o_b @ w_b)) @ w_out
    x = layer_norm(ALPHA * x + mix, ln1_g, ln1_b)
    h = jnp.square(jax.nn.relu(x @ w_up)) @ w_down
    x = layer_norm(ALPHA * x + h, ln2_g, ln2_b)
    return x + jax.nn.sigmoid(x @ w_pg) * (p @ w_p)


def prompt_mixers(qn, qr, ckv, kr, dq, dk, dv, table, w_uk, w_uv, lam, lam_init, subln_g):
    b, s = qn.shape[0], qn.shape[1]
    k_pos = jnp.arange(s)

    def block(i):
        q0 = i * Q_BLOCK
        q_pos = q0 + jnp.arange(Q_BLOCK)
        mask = q_pos[:, None] >= k_pos[None, :]
        bias = rel_bias(table, q_pos, k_pos)
        sl = lambda t: lax.dynamic_slice_in_dim(t, q0, Q_BLOCK, axis=1)
        o_a = mla_core(sl(qn), sl(qr), ckv, kr, mask, w_uk, w_uv)
        o_b = diff_core(sl(dq), dk, dv, bias, mask, lam, lam_init, subln_g)
        return o_a, o_b

    o_a, o_b = lax.map(block, jnp.arange(s // Q_BLOCK))
    o_a = jnp.moveaxis(o_a, 0, 1).reshape(b, s, H_A * DV_A)
    o_b = jnp.moveaxis(o_b, 0, 1).reshape(b, s, H_B * DV_B)
    return o_a, o_b


def sample_mixers(qn, qr, ckv, kr, dq, dk, dv, page_table, lat_pool, kr_pool, k_pool, v_pool,
                  table, w_uk, w_uv, lam, lam_init, subln_g):
    q_pos = PAST_LEN + jnp.arange(DEC_SEQ)
    k_pos = jnp.arange(PAST_LEN + DEC_SEQ)
    mask = q_pos[:, None] >= k_pos[None, :]
    bias = rel_bias(table, q_pos, k_pos)

    def one(args):
        pt, qn_i, qr_i, ckv_i, kr_i, dq_i, dk_i, dv_i = args
        ckv_all = jnp.concatenate([lat_pool[pt].reshape(-1, KV_LORA), ckv_i], 0)
        kr_all = jnp.concatenate([kr_pool[pt].reshape(-1, DR), kr_i], 0)
        dk_all = jnp.concatenate([k_pool[pt].reshape(-1, H_B, 2, DH), dk_i], 0)
        dv_all = jnp.concatenate([v_pool[pt].reshape(-1, H_B, DV_B), dv_i], 0)
        o_a = mla_core(qn_i[None], qr_i[None], ckv_all[None], kr_all[None], mask, w_uk, w_uv)[0]
        o_b = diff_core(dq_i[None], dk_all[None], dv_all[None], bias, mask, lam, lam_init, subln_g)[0]
        return o_a, o_b

    return lax.map(one, (page_table, qn, qr, ckv, kr, dq, dk, dv))


def setup_inputs(seed: int = 0) -> dict:
    key = jax.random.key(seed)
    ks = jax.random.split(key, 32)
    n_pages = PAST_LEN // PAGE_SIZE
    n_used = DEC_BATCH * n_pages
    n_pool = n_used + n_used // 4
    nrm = lambda k, shape, scale=1.0: jax.random.normal(k, shape, jnp.float32) * scale
    gain = lambda k, shape: 1.0 + nrm(k, shape, 0.02)
    x_prompt = nrm(ks[0], (BATCH, SEQ, D_MODEL))
    x_sample = nrm(ks[1], (DEC_BATCH, DEC_SEQ, D_MODEL))
    cache_mla_latent = nrm(ks[2], (DEPTH, n_pool, PAGE_SIZE, KV_LORA))
    cache_mla_krope = nrm(ks[3], (DEPTH, n_pool, PAGE_SIZE, DR))
    cache_diff_k = nrm(ks[4], (DEPTH, n_pool, PAGE_SIZE, H_B, 2 * DH))
    cache_diff_v = nrm(ks[5], (DEPTH, n_pool, PAGE_SIZE, H_B, DV_B), BETA)
    page_table = jax.random.permutation(ks[6], n_pool)[:n_used].reshape(DEC_BATCH, n_pages).astype(jnp.int32)
    p_prompt = nrm(ks[7], (DEPTH, BATCH, SEQ, D_PLE))
    p_sample = nrm(ks[8], (DEPTH, DEC_BATCH, DEC_SEQ, D_PLE))
    rel_bias_table = nrm(ks[9], (N_BUCKETS, H_B), 0.5)
    v_cols = H_A * DN + H_A * DR + KV_LORA + DR + 2 * H_B * 2 * DH
    col_scale = jnp.concatenate([jnp.ones((v_cols,), jnp.float32),
                                 jnp.full((H_B * DV_B,), BETA, jnp.float32),
                                 jnp.ones((2 * D_MODEL,), jnp.float32)])
    w_in = nrm(ks[10], (DEPTH, D_MODEL, D_IN), D_MODEL ** -0.5) * col_scale
    kv_norm_g = gain(ks[11], (DEPTH, KV_LORA))
    w_uk = nrm(ks[12], (DEPTH, KV_LORA, H_A, DN), KV_LORA ** -0.5)
    w_uv = nrm(ks[13], (DEPTH, KV_LORA, H_A, DV_A), BETA * KV_LORA ** -0.5)
    lam_q1 = nrm(ks[14], (DEPTH, DH), 0.1)
    lam_k1 = nrm(ks[15], (DEPTH, DH), 0.1)
    lam_q2 = nrm(ks[16], (DEPTH, DH), 0.1)
    lam_k2 = nrm(ks[17], (DEPTH, DH), 0.1)
    subln_g = gain(ks[18], (DEPTH, DV_B))
    w_a = nrm(ks[19], (DEPTH, H_A * DV_A, D_MODEL), BETA * (H_A * DV_A) ** -0.5)
    w_b = nrm(ks[20], (DEPTH, H_B * DV_B, D_MODEL), BETA * (H_B * DV_B) ** -0.5)
    w_out = nrm(ks[21], (DEPTH, D_MODEL, D_MODEL), BETA * D_MODEL ** -0.5)
    ln1_g = gain(ks[22], (DEPTH, D_MODEL))
    ln1_b = nrm(ks[23], (DEPTH, D_MODEL), 0.02)
    w_up = nrm(ks[24], (DEPTH, D_MODEL, D_FF), BETA * D_MODEL ** -0.5)
    w_down = nrm(ks[25], (DEPTH, D_FF, D_MODEL), BETA * D_FF ** -0.5)
    ln2_g = gain(ks[26], (DEPTH, D_MODEL))
    ln2_b = nrm(ks[27], (DEPTH, D_MODEL), 0.02)
    w_p = nrm(ks[28], (DEPTH, D_PLE, D_MODEL), D_PLE ** -0.5)
    w_pg = nrm(ks[29], (DEPTH, D_MODEL, D_MODEL), D_MODEL ** -0.5)
    return {'x_prompt': x_prompt, 'x_sample': x_sample,
            'cache_mla_latent': cache_mla_latent, 'cache_mla_krope': cache_mla_krope,
            'cache_diff_k': cache_diff_k, 'cache_diff_v': cache_diff_v,
            'page_table': page_table, 'p_prompt': p_prompt, 'p_sample': p_sample,
            'rel_bias_table': rel_bias_table, 'w_in': w_in, 'kv_norm_g': kv_norm_g,
            'w_uk': w_uk, 'w_uv': w_uv, 'lam_q1': lam_q1, 'lam_k1': lam_k1,
            'lam_q2': lam_q2, 'lam_k2': lam_k2, 'subln_g': subln_g,
            'w_a': w_a, 'w_b': w_b, 'w_out': w_out, 'ln1_g': ln1_g, 'ln1_b': ln1_b,
            'w_up': w_up, 'w_down': w_down, 'ln2_g': ln2_g, 'ln2_b': ln2_b,
            'w_p': w_p, 'w_pg': w_pg}


def reference(x_prompt, x_sample, cache_mla_latent, cache_mla_krope, cache_diff_k, cache_diff_v,
              page_table, p_prompt, p_sample, rel_bias_table, w_in, kv_norm_g, w_uk, w_uv,
              lam_q1, lam_k1, lam_q2, lam_k2, subln_g, w_a, w_b, w_out, ln1_g, ln1_b,
              w_up, w_down, ln2_g, ln2_b, w_p, w_pg):
    pos_p = jnp.arange(SEQ)
    pos_s = PAST_LEN + jnp.arange(DEC_SEQ)
    xp, xs = x_prompt, x_sample
    lat_p, lat_s, kr_p, kr_s, dk_p, dk_s, dv_p, dv_s = [], [], [], [], [], [], [], []
    for l in range(DEPTH):
        lam_init = 0.8 - 0.6 * math.exp(-0.3 * l)
        lam = (jnp.exp(jnp.sum(lam_q1[l].astype(jnp.float32) * lam_k1[l].astype(jnp.float32)))
               - jnp.exp(jnp.sum(lam_q2[l].astype(jnp.float32) * lam_k2[l].astype(jnp.float32)))
               + lam_init)
        qn, qr, ckv, kr, dq, dk, dv, ga, gb = mixer_inputs(xp, pos_p, w_in[l], kv_norm_g[l])
        o_a, o_b = prompt_mixers(qn, qr, ckv, kr, dq, dk, dv, rel_bias_table, w_uk[l], w_uv[l],
                                 lam, lam_init, subln_g[l])
        lat_p.append(ckv)
        kr_p.append(kr)
        dk_p.append(dk.reshape(BATCH, SEQ, H_B, 2 * DH))
        dv_p.append(dv)
        xp = merge_and_channel(xp, o_a, o_b, ga, gb, p_prompt[l], w_a[l], w_b[l], w_out[l],
                               ln1_g[l], ln1_b[l], w_up[l], w_down[l], ln2_g[l], ln2_b[l], w_p[l], w_pg[l])
        qn, qr, ckv, kr, dq, dk, dv, ga, gb = mixer_inputs(xs, pos_s, w_in[l], kv_norm_g[l])
        o_a, o_b = sample_mixers(qn, qr, ckv, kr, dq, dk, dv, page_table,
                                 cache_mla_latent[l], cache_mla_krope[l], cache_diff_k[l], cache_diff_v[l],
                                 rel_bias_table, w_uk[l], w_uv[l], lam, lam_init, subln_g[l])
        lat_s.append(ckv)
        kr_s.append(kr)
        dk_s.append(dk.reshape(DEC_BATCH, DEC_SEQ, H_B, 2 * DH))
        dv_s.append(dv)
        xs = merge_and_channel(xs, o_a, o_b, ga, gb, p_sample[l], w_a[l], w_b[l], w_out[l],
                               ln1_g[l], ln1_b[l], w_up[l], w_down[l], ln2_g[l], ln2_b[l], w_p[l], w_pg[l])
    return (xp, xs,
            jnp.stack(lat_p), jnp.stack(lat_s),
            jnp.stack(kr_p), jnp.stack(kr_s),
            jnp.stack(dk_p), jnp.stack(dk_s),
            jnp.stack(dv_p), jnp.stack(dv_s))
```

```python
import functools
import math

import jax
import jax.numpy as jnp
from jax import lax
from jax.experimental import pallas as pl
from jax.experimental.pallas import tpu as pltpu

F32 = jnp.float32
BF16 = jnp.bfloat16
I32 = jnp.int32

D_MODEL = 1024
H_A, DN, DR, DV_A, KV_LORA = 8, 64, 32, 64, 256
H_B, DH, DV_B = 8, 32, 64
MLA_SCALE = (DN + DR) ** -0.5
DIFF_SCALE = DH ** -0.5
ROPE_THETA = 10000.0
N_BUCKETS, MAX_DIST = 32, 128
D_FF = 4 * D_MODEL
DEPTH_ALPHA = lambda depth: (2 * depth) ** 0.25
EPS = 1e-5
NEG = -1e30

LANES = 128
HEAD_PAD = LANES
ROPE_LO = DN
VMEM_LIMIT = 56 * 1024 * 1024

QC0, QC1 = 0, H_A * HEAD_PAD
CKV0, CKV1 = QC1, QC1 + KV_LORA
KR0, KR1 = CKV1, CKV1 + HEAD_PAD
DQ0, DQ1 = KR1, KR1 + H_B * 2 * DH
DK0, DK1 = DQ1, DQ1 + H_B * 2 * DH
DV0, DV1 = DK1, DK1 + H_B * DV_B
N_ATT = DV1


def _dot(a, b):
    return jnp.dot(a, b, preferred_element_type=F32)


def _dot_nt(a, b):
    return lax.dot_general(a, b, (((1,), (1,)), ((), ())), preferred_element_type=F32)


def _const_spec(shape):
    nd = len(shape)
    return pl.BlockSpec(shape, lambda *_: (0,) * nd)


def _params(sem):
    return pltpu.CompilerParams(dimension_semantics=sem, vmem_limit_bytes=VMEM_LIMIT)


def _rope128(x, c, sa, sb):
    return x * c + pltpu.roll(x, LANES - DR // 2, 1) * sa + pltpu.roll(x, DR // 2, 1) * sb


def _proj_body(x_ref, w_ref, c_ref, sa_ref, sb_ref, kvg_ref, wuk_ref, wuv_ref,
               qc_ref, dq_ref, ckv_ref, kr_ref, dk_ref, dv_ref, kc_ref, vm_ref, dkb_ref, dvb_ref):
    xb = x_ref[...].astype(BF16)
    c, sa, sb = c_ref[...], sa_ref[...], sb_ref[...]
    zq = _dot(xb, w_ref[:, QC0:QC1])
    for h in range(H_A):
        sl = slice(h * HEAD_PAD, (h + 1) * HEAD_PAD)
        qc_ref[:, sl] = _rope128(zq[:, sl], c, sa, sb).astype(BF16)
    zc = _dot(xb, w_ref[:, CKV0:CKV1])
    ckv = zc * lax.rsqrt(jnp.mean(zc * zc, -1, keepdims=True) + EPS) * kvg_ref[...]
    ckv_ref[...] = ckv
    cb = ckv.astype(BF16)
    zk = _rope128(_dot(xb, w_ref[:, KR0:KR1]), c, sa, sb)
    kr_ref[...] = zk[:, ROPE_LO:ROPE_LO + DR]
    kn = _dot(cb, wuk_ref[...])
    for h in range(H_A):
        sl = slice(h * HEAD_PAD, (h + 1) * HEAD_PAD)
        kc_ref[:, sl] = (kn[:, sl] + zk).astype(BF16)
    vm_ref[...] = _dot(cb, wuv_ref[...]).astype(BF16)
    dq_ref[...] = _dot(xb, w_ref[:, DQ0:DQ1]).astype(BF16)
    zdk = _dot(xb, w_ref[:, DK0:DK1])
    dk_ref[...] = zdk
    dkb_ref[...] = zdk.astype(BF16)
    zdv = _dot(xb, w_ref[:, DV0:DV1])
    dv_ref[...] = zdv
    dvb_ref[...] = zdv.astype(BF16)


def _proj(x, w_att, tabs, kvg, wuk_pad, wuv2d, tm):
    t = x.shape[0]
    c, sa, sb = tabs
    tab_blocks = c.shape[0] // tm
    row = lambda w: pl.BlockSpec((tm, w), lambda i: (i, 0))
    tab = pl.BlockSpec((tm, LANES), lambda i: (i % tab_blocks, 0))
    out_shapes = [
        jax.ShapeDtypeStruct((t, H_A * HEAD_PAD), BF16),
        jax.ShapeDtypeStruct((t, H_B * 2 * DH), BF16),
        jax.ShapeDtypeStruct((t, KV_LORA), F32),
        jax.ShapeDtypeStruct((t, DR), F32),
        jax.ShapeDtypeStruct((t, H_B * 2 * DH), F32),
        jax.ShapeDtypeStruct((t, H_B * DV_B), F32),
        jax.ShapeDtypeStruct((t, H_A * HEAD_PAD), BF16),
        jax.ShapeDtypeStruct((t, H_A * DV_A), BF16),
        jax.ShapeDtypeStruct((t, H_B * 2 * DH), BF16),
        jax.ShapeDtypeStruct((t, H_B * DV_B), BF16),
    ]
    return pl.pallas_call(
        _proj_body,
        grid=(t // tm,),
        in_specs=[row(D_MODEL), _const_spec(w_att.shape), tab, tab, tab,
                  _const_spec(kvg.shape), _const_spec(wuk_pad.shape), _const_spec(wuv2d.shape)],
        out_specs=[row(s.shape[1]) for s in out_shapes],
        out_shape=out_shapes,
        compiler_params=_params(("arbitrary",)),
        name="proj",
    )(x, w_att, c, sa, sb, kvg, wuk_pad, wuv2d)


def _lam_value(lq1, lk1, lq2, lk2, lam_init):
    a = jnp.exp(jnp.sum(lq1[...] * lk1[...], -1, keepdims=True))
    b = jnp.exp(jnp.sum(lq2[...] * lk2[...], -1, keepdims=True))
    return a - b + lam_init


def _online(m_ref, l_ref, acc_ref, rows, arows, s, v, v_transposed=False):
    m_prev = m_ref[rows]
    m_new = jnp.maximum(m_prev, jnp.max(s, -1, keepdims=True))
    alpha = jnp.exp(m_prev - m_new)
    p = jnp.exp(s - m_new)
    l_ref[rows] = alpha * l_ref[rows] + jnp.sum(p, -1, keepdims=True)
    pv = _dot_nt(p.astype(BF16), v) if v_transposed else _dot(p.astype(BF16), v)
    acc_ref[arows] = alpha * acc_ref[arows] + pv
    m_ref[rows] = m_new


def _attn_body(table_ref, bkt_ref, qc_ref, dq_ref, kc_ref, vm_ref, dk_ref, dv_ref,
               lq1, lk1, lq2, lk2, g_ref, oa_ref, ob_ref,
               bias_ref, qd_ref, m_ref, l_ref, acca_ref, accd_ref, *, lam_init, tq):
    b = pl.program_id(0)
    i = pl.program_id(1)

    @pl.when((b == 0) & (i == 0))
    def _build_bias():
        bias_ref[...] = jnp.zeros(bias_ref.shape, F32)

        def per_bucket(n, carry):
            for t in range(3):
                hit = bkt_ref[t] == n
                for h in range(H_B):
                    bias_ref[t, h] = jnp.where(hit, table_ref[n, h], bias_ref[t, h])
            return carry

        lax.fori_loop(0, N_BUCKETS, per_bucket, 0)

    m_ref[...] = jnp.full(m_ref.shape, NEG, F32)
    l_ref[...] = jnp.zeros(l_ref.shape, F32)
    acca_ref[...] = jnp.zeros(acca_ref.shape, F32)
    accd_ref[...] = jnp.zeros(accd_ref.shape, F32)

    lane = lax.broadcasted_iota(I32, (tq, 2 * DH), 1)
    for h in range(H_B):
        qh = dq_ref[:, h * 2 * DH:(h + 1) * 2 * DH]
        zero = jnp.zeros_like(qh)
        qd_ref[2 * h] = jnp.where(lane < DH, qh, zero)
        qd_ref[2 * h + 1] = jnp.where(lane >= DH, qh, zero)

    def step(j, t, masked):
        ks = pl.ds(pl.multiple_of(j * tq, tq), tq)
        if masked:
            causal = (lax.broadcasted_iota(I32, (tq, tq), 0) >= lax.broadcasted_iota(I32, (tq, tq), 1))
        for h in range(H_A):
            s = _dot_nt(qc_ref[:, h * HEAD_PAD:(h + 1) * HEAD_PAD], kc_ref[ks, h * HEAD_PAD:(h + 1) * HEAD_PAD])
            if masked:
                s = jnp.where(causal, s, NEG)
            _online(m_ref, l_ref, acca_ref, h, h, s, vm_ref[ks, h * DV_A:(h + 1) * DV_A])
        for h in range(H_B):
            kh = dk_ref[ks, h * 2 * DH:(h + 1) * 2 * DH]
            vh = dv_ref[ks, h * DV_B:(h + 1) * DV_B]
            bias = bias_ref[t, h]
            for jm in range(2):
                s = _dot_nt(qd_ref[2 * h + jm], kh) + bias
                if masked:
                    s = jnp.where(causal, s, NEG)
                _online(m_ref, l_ref, accd_ref, H_A + 2 * h + jm, 2 * h + jm, s, vh)

    def far_step(j, carry):
        step(j, jnp.where(i - j >= 2, 0, 1), False)
        return carry

    lax.fori_loop(0, i, far_step, 0)
    step(i, 2, True)

    lam = _lam_value(lq1, lk1, lq2, lk2, lam_init)
    g = g_ref[...]
    for h in range(H_A):
        oa_ref[:, h * DV_A:(h + 1) * DV_A] = (acca_ref[h] / l_ref[h]).astype(oa_ref.dtype)
    for h in range(H_B):
        o = (accd_ref[2 * h] / l_ref[H_A + 2 * h]
             - lam * (accd_ref[2 * h + 1] / l_ref[H_A + 2 * h + 1]))
        o = o * lax.rsqrt(jnp.mean(o * o, -1, keepdims=True) + EPS) * g * (1.0 - lam_init)
        ob_ref[:, h * DV_B:(h + 1) * DV_B] = o.astype(ob_ref.dtype)


def _prompt_attention(table, bkt, qc, dq, kc, vm, dkb, dvb, lams, g, lam_init, batch, seq, tq):
    r3 = lambda a: a.reshape(batch, seq, a.shape[-1])
    qc, dq, kc, vm, dkb, dvb = map(r3, (qc, dq, kc, vm, dkb, dvb))
    qblk = lambda w: pl.BlockSpec((None, tq, w), lambda b, i: (b, i, 0))
    full = lambda w: pl.BlockSpec((None, seq, w), lambda b, i: (b, 0, 0))
    small = [_const_spec(a.shape) for a in lams] + [_const_spec(g.shape)]
    oa, ob = pl.pallas_call(
        functools.partial(_attn_body, lam_init=lam_init, tq=tq),
        grid=(batch, seq // tq),
        in_specs=[pl.BlockSpec(memory_space=pltpu.SMEM), _const_spec(bkt.shape),
                  qblk(qc.shape[-1]), qblk(dq.shape[-1]),
                  full(kc.shape[-1]), full(vm.shape[-1]), full(dkb.shape[-1]), full(dvb.shape[-1])] + small,
        out_specs=[qblk(H_A * DV_A), qblk(H_B * DV_B)],
        out_shape=[jax.ShapeDtypeStruct((batch, seq, H_A * DV_A), BF16),
                   jax.ShapeDtypeStruct((batch, seq, H_B * DV_B), BF16)],
        scratch_shapes=[pltpu.VMEM((3, H_B, tq, tq), F32),
                        pltpu.VMEM((2 * H_B, tq, 2 * DH), BF16),
                        pltpu.VMEM((H_A + 2 * H_B, tq, 1), F32),
                        pltpu.VMEM((H_A + 2 * H_B, tq, 1), F32),
                        pltpu.VMEM((H_A, tq, DV_A), F32),
                        pltpu.VMEM((2 * H_B, tq, DV_B), F32)],
        compiler_params=_params(("arbitrary", "arbitrary")),
        name="prompt_attn",
    )(table, bkt, qc, dq, kc, vm, dkb, dvb, *lams, g)
    return oa.reshape(batch * seq, -1), ob.reshape(batch * seq, -1)


NEW_PAD = LANES


def _dec_body(pt_ref, qc_ref, dq_ref, ckvn_ref, krn_ref, dkn_ref, dvn_ref,
              wuk_ref, sel_ref, wuv_ref, tcol_ref, bktp_ref, bktn_ref,
              lq1, lk1, lq2, lk2, g_ref, *rest, lam_init, n_pg, page, n_q):
    lat_refs, kr_refs = rest[0:n_pg], rest[n_pg:2 * n_pg]
    k_refs, v_refs = rest[2 * n_pg:3 * n_pg], rest[3 * n_pg:4 * n_pg]
    oa_ref, ob_ref = rest[4 * n_pg:4 * n_pg + 2]
    (bias_ref, biasn_ref, kb_ref, krt_ref, kt_ref, vt_ref, nb_ref, nk_ref, nv_ref,
     qbd_ref, qm_ref, qd_ref, m_ref, l_ref, acca_ref, accd_ref) = rest[4 * n_pg + 2:]
    del pt_ref
    b = pl.program_id(0)
    c = pl.program_id(1)
    last = pl.num_programs(1) - 1
    rows_a = n_q * H_A
    rows_d = 2 * n_q * H_B

    @pl.when((b == 0) & (c == 0))
    def _once():
        bias_ref[...] = jnp.zeros(bias_ref.shape, F32)
        biasn_ref[...] = jnp.zeros(biasn_ref.shape, F32)

        def per_bucket(n, carry):
            col = tcol_ref[n]
            for t in range(2):
                bias_ref[t] = jnp.where(bktp_ref[t] == n, col, bias_ref[t])
            biasn_ref[...] = jnp.where(bktn_ref[...] == n, col, biasn_ref[...])
            return carry

        lax.fori_loop(0, N_BUCKETS, per_bucket, 0)

    @pl.when(c == 0)
    def _per_sequence():
        m_ref[...] = jnp.full(m_ref.shape, NEG, F32)
        l_ref[...] = jnp.zeros(l_ref.shape, F32)
        acca_ref[...] = jnp.zeros(acca_ref.shape, F32)
        accd_ref[...] = jnp.zeros(accd_ref.shape, F32)
        head_a = lax.broadcasted_iota(I32, (H_A, H_A * HEAD_PAD), 1) // HEAD_PAD
        row_a = lax.broadcasted_iota(I32, (H_A, H_A * HEAD_PAD), 0)
        lane_d = lax.broadcasted_iota(I32, (H_B, H_B * 2 * DH), 1)
        row_d = lax.broadcasted_iota(I32, (H_B, H_B * 2 * DH), 0)
        qcf = qc_ref[...].astype(F32)
        dqf = dq_ref[...].astype(F32)
        for q in range(n_q):
            qrow = jnp.broadcast_to(qcf[q:q + 1, :], (H_A, H_A * HEAD_PAD))
            qbd_ref[q * H_A:(q + 1) * H_A, :] = jnp.where(head_a == row_a, qrow, 0.0).astype(BF16)
            drow = jnp.broadcast_to(dqf[q:q + 1, :], (H_B, H_B * 2 * DH))
            for jm in range(2):
                keep = (lane_d // (2 * DH) == row_d) & ((lane_d % (2 * DH)) // DH == jm)
                r0 = jm * n_q * H_B + q * H_B
                qd_ref[r0:r0 + H_B, :] = jnp.where(keep, drow, 0.0).astype(BF16)
        qbd = qbd_ref[...]
        qm_ref[:, 0:KV_LORA] = _dot_nt(qbd, wuk_ref[...]).astype(BF16)
        qm_ref[:, KV_LORA:] = _dot(qbd, sel_ref[...]).astype(BF16)

    for p in range(n_pg):
        rs = slice(p * page, (p + 1) * page)
        kb_ref[rs, :] = lat_refs[p][...].astype(BF16)
        krt_ref[:, rs] = kr_refs[p][...].astype(BF16)
        kt_ref[:, rs] = k_refs[p][...].astype(BF16)
        vt_ref[:, rs] = v_refs[p][...].astype(BF16)

    ra = pl.ds(0, rows_a)
    rd = pl.ds(rows_a, rows_d)
    s = (_dot_nt(qm_ref[:, 0:KV_LORA], kb_ref[...])
         + _dot(qm_ref[:, KV_LORA:KV_LORA + DR], krt_ref[...]))
    _online(m_ref, l_ref, acca_ref, ra, slice(None), s, kb_ref[...])
    t = jnp.where(c == last, 1, 0)
    sd = _dot(qd_ref[...], kt_ref[...]) + bias_ref[t]
    _online(m_ref, l_ref, accd_ref, rd, slice(None), sd, vt_ref[...], v_transposed=True)

    @pl.when(c == last)
    def _finish():
        nb_ref[...] = jnp.zeros(nb_ref.shape, F32)
        nk_ref[...] = jnp.zeros(nk_ref.shape, F32)
        nv_ref[...] = jnp.zeros(nv_ref.shape, F32)
        nb_ref[0:n_q, 0:KV_LORA] = ckvn_ref[...]
        nb_ref[0:n_q, KV_LORA:KV_LORA + DR] = krn_ref[...]
        nk_ref[0:n_q, :] = dkn_ref[...]
        nv_ref[0:n_q, :] = dvn_ref[...]
        nb = nb_ref[...].astype(BF16)
        key = lax.broadcasted_iota(I32, (rows_a, NEW_PAD), 1)
        qry = lax.broadcasted_iota(I32, (rows_a, NEW_PAD), 0) // H_A
        s = jnp.where(key <= qry, _dot_nt(qm_ref[...], nb), NEG)
        _online(m_ref, l_ref, acca_ref, ra, slice(None), s, nb[:, 0:KV_LORA])
        key = lax.broadcasted_iota(I32, (rows_d, NEW_PAD), 1)
        qry = (lax.broadcasted_iota(I32, (rows_d, NEW_PAD), 0) % (n_q * H_B)) // H_B
        sd = _dot_nt(qd_ref[...], nk_ref[...].astype(BF16)) + biasn_ref[...]
        sd = jnp.where(key <= qry, sd, NEG)
        _online(m_ref, l_ref, accd_ref, rd, slice(None), sd, nv_ref[...].astype(BF16))

        o_lat = (acca_ref[...] / l_ref[ra]).astype(BF16)
        ra_full = _dot(o_lat, wuv_ref[...])
        own_a = (lax.broadcasted_iota(I32, (rows_a, H_A * DV_A), 1) // DV_A
                 == lax.broadcasted_iota(I32, (rows_a, H_A * DV_A), 0) % H_A)
        ra_full = jnp.where(own_a, ra_full, 0.0)
        lam = _lam_value(lq1, lk1, lq2, lk2, lam_init)
        half = n_q * H_B
        o1 = accd_ref[0:half, :] / l_ref[pl.ds(rows_a, half)]
        o2 = accd_ref[half:2 * half, :] / l_ref[pl.ds(rows_a + half, half)]
        own_d = (lax.broadcasted_iota(I32, (half, H_B * DV_B), 1) // DV_B
                 == lax.broadcasted_iota(I32, (half, H_B * DV_B), 0) % H_B)
        o = jnp.where(own_d, o1 - lam * o2, 0.0)
        ms = jnp.sum(o * o, -1, keepdims=True) * (1.0 / DV_B)
        o = o * lax.rsqrt(ms + EPS) * g_ref[...] * (1.0 - lam_init)
        for q in range(n_q):
            oa_ref[q:q + 1, :] = jnp.sum(ra_full[q * H_A:(q + 1) * H_A, :], 0, keepdims=True)
            ob_ref[q:q + 1, :] = jnp.sum(o[q * H_B:(q + 1) * H_B, :], 0, keepdims=True)


def _sample_attention(layer, page_table, caches, qc, dq, ckv, kr, dk, dv, wuk_pad, sel, wuv2d,
                      tcol, bktp, bktn, lams, g_tiled, lam_init, n_pg):
    lat_pool, kr_pool, k_pool, v_pool = caches
    bsz, n_pages = page_table.shape
    n_q = qc.shape[0] // bsz
    page = lat_pool.shape[2]
    assert page == LANES
    ck = n_pg * page
    r3 = lambda a: a.reshape(bsz, n_q, a.shape[-1])
    per_seq = [r3(a) for a in (qc, dq, ckv, kr, dk, dv)]
    seq_spec = lambda a: pl.BlockSpec((None, n_q, a.shape[-1]), lambda b, c, pt: (b, 0, 0))
    cst = lambda a: pl.BlockSpec(a.shape, lambda b, c, pt, nd=a.ndim: (0,) * nd)

    def page_specs(pool):
        r, w = pool.shape[2:]
        return [pl.BlockSpec((None, None, r, w),
                             lambda b, c, pt, p=p: (layer, pt[b, c * n_pg + p], 0, 0))
                for p in range(n_pg)]

    consts = [wuk_pad, sel, wuv2d, tcol, bktp, bktn, *lams, g_tiled]
    rows_a, rows_d = n_q * H_A, 2 * n_q * H_B
    lat_w = KV_LORA + LANES
    kw = k_pool.shape[2]
    grid_spec = pltpu.PrefetchScalarGridSpec(
        num_scalar_prefetch=1,
        grid=(bsz, n_pages // n_pg),
        in_specs=[seq_spec(a) for a in per_seq] + [cst(a) for a in consts]
        + page_specs(lat_pool) + page_specs(kr_pool) + page_specs(k_pool) + page_specs(v_pool),
        out_specs=[pl.BlockSpec((None, n_q, H_A * DV_A), lambda b, c, pt: (b, 0, 0)),
                   pl.BlockSpec((None, n_q, H_B * DV_B), lambda b, c, pt: (b, 0, 0))],
        scratch_shapes=[pltpu.VMEM((2, rows_d, ck), F32),
                        pltpu.VMEM((rows_d, NEW_PAD), F32),
                        pltpu.VMEM((ck, KV_LORA), BF16),
                        pltpu.VMEM((DR, ck), BF16),
                        pltpu.VMEM((kw, ck), BF16),
                        pltpu.VMEM((kw, ck), BF16),
                        pltpu.VMEM((NEW_PAD, lat_w), F32),
                        pltpu.VMEM((NEW_PAD, kw), F32),
                        pltpu.VMEM((NEW_PAD, kw), F32),
                        pltpu.VMEM((rows_a, H_A * HEAD_PAD), BF16),
                        pltpu.VMEM((rows_a, lat_w), BF16),
                        pltpu.VMEM((rows_d, kw), BF16),
                        pltpu.VMEM((rows_a + rows_d, 1), F32),
                        pltpu.VMEM((rows_a + rows_d, 1), F32),
                        pltpu.VMEM((rows_a, KV_LORA), F32),
                        pltpu.VMEM((rows_d, kw), F32)],
    )
    oa, ob = pl.pallas_call(
        functools.partial(_dec_body, lam_init=lam_init, n_pg=n_pg, page=page, n_q=n_q),
        grid_spec=grid_spec,
        out_shape=[jax.ShapeDtypeStruct((bsz, n_q, H_A * DV_A), F32),
                   jax.ShapeDtypeStruct((bsz, n_q, H_B * DV_B), F32)],
        compiler_params=_params(("arbitrary", "arbitrary")),
        name="sample_attn",
    )(page_table, *per_seq, *consts,
      *([lat_pool] * n_pg), *([kr_pool] * n_pg), *([k_pool] * n_pg), *([v_pool] * n_pg))
    return oa.reshape(bsz * n_q, -1), ob.reshape(bsz * n_q, -1)


def _layer_norm(y, g, b):
    yc = y - jnp.mean(y, -1, keepdims=True)
    var = jnp.mean(yc * yc, -1, keepdims=True)
    return yc * lax.rsqrt(var + EPS) * g + b


def _merge_body(x_ref, oa_ref, ob_ref, wg_ref, wa_ref, wb_ref, wo_ref, g_ref, b_ref, y_ref, *, alpha):
    x = x_ref[...]
    xb = x.astype(BF16)
    ga = jax.nn.sigmoid(_dot(xb, wg_ref[:, 0:D_MODEL]))
    gb = jax.nn.sigmoid(_dot(xb, wg_ref[:, D_MODEL:2 * D_MODEL]))
    mix = (ga * _dot(oa_ref[...].astype(BF16), wa_ref[...])
           + gb * _dot(ob_ref[...].astype(BF16), wb_ref[...]))
    y = alpha * x + _dot(mix.astype(BF16), wo_ref[...])
    y_ref[...] = _layer_norm(y, g_ref[...], b_ref[...])


def _merge(x, oa, ob, wg, wa, wb, wo, g, b, alpha, tm):
    t = x.shape[0]
    row = lambda a: pl.BlockSpec((tm, a.shape[1]), lambda i: (i, 0))
    ws = [wg, wa, wb, wo, g, b]
    return pl.pallas_call(
        functools.partial(_merge_body, alpha=alpha),
        grid=(t // tm,),
        in_specs=[row(x), row(oa), row(ob)] + [_const_spec(w.shape) for w in ws],
        out_specs=row(x),
        out_shape=jax.ShapeDtypeStruct(x.shape, F32),
        compiler_params=_params(("arbitrary",)),
        name="merge",
    )(x, oa, ob, *ws)


FF_CHUNK = 512


def _ffn_body(x_ref, p_ref, wu_ref, wd_ref, g_ref, b_ref, wpg_ref, wp_ref, y_ref, *, alpha):
    x = x_ref[...]
    xb = x.astype(BF16)
    h = jnp.zeros(x.shape, F32)
    for k in range(D_FF // FF_CHUNK):
        sl = slice(k * FF_CHUNK, (k + 1) * FF_CHUNK)
        u = jnp.maximum(_dot(xb, wu_ref[:, sl]), 0.0)
        h = h + _dot((u * u).astype(BF16), wd_ref[sl, :])
    y = _layer_norm(alpha * x + h, g_ref[...], b_ref[...])
    gate = jax.nn.sigmoid(_dot(y.astype(BF16), wpg_ref[...]))
    y_ref[...] = y + gate * _dot(p_ref[...].astype(BF16), wp_ref[...])


def _ffn(x, p, wu, wd, g, b, wpg, wp, alpha, tm):
    t = x.shape[0]
    row = lambda a: pl.BlockSpec((tm, a.shape[1]), lambda i: (i, 0))
    ws = [wu, wd, g, b, wpg, wp]
    return pl.pallas_call(
        functools.partial(_ffn_body, alpha=alpha),
        grid=(t // tm,),
        in_specs=[row(x), row(p)] + [_const_spec(w.shape) for w in ws],
        out_specs=row(x),
        out_shape=jax.ShapeDtypeStruct(x.shape, F32),
        compiler_params=_params(("arbitrary",)),
        name="ffn",
    )(x, p, *ws)


def _t5_bucket(n):
    exact = N_BUCKETS // 2
    nf = jnp.maximum(n, 1).astype(F32)
    large = exact + (jnp.log(nf / exact) / math.log(MAX_DIST / exact) * (N_BUCKETS - exact)).astype(I32)
    large = jnp.minimum(large, N_BUCKETS - 1)
    return jnp.where(n < exact, n, large)


def _rope_tables(pos):
    half = DR // 2
    inv = ROPE_THETA ** (-jnp.arange(half, dtype=F32) / half)
    ang = pos.astype(F32)[:, None] * inv[None, :]
    cos, sin = jnp.cos(ang), jnp.sin(ang)
    n = pos.shape[0]
    ones = jnp.ones((n, ROPE_LO), F32)
    z = lambda w: jnp.zeros((n, w), F32)
    pad = HEAD_PAD - ROPE_LO - DR
    c = jnp.concatenate([ones, cos, cos, z(pad)], 1)
    sa = jnp.concatenate([z(ROPE_LO), -sin, z(half), z(pad)], 1)
    sb = jnp.concatenate([z(ROPE_LO), z(half), sin, z(pad)], 1)
    return c, sa, sb


def _layer_weights(w_in, w_uk, w_uv):
    d = w_in.shape[0]
    o = 0
    qn = w_in[:, o:o + H_A * DN].reshape(d, H_A, DN) * MLA_SCALE; o += H_A * DN
    qr = w_in[:, o:o + H_A * DR].reshape(d, H_A, DR) * MLA_SCALE; o += H_A * DR
    w_ckv = w_in[:, o:o + KV_LORA]; o += KV_LORA
    kr = w_in[:, o:o + DR]; o += DR
    w_dq = w_in[:, o:o + H_B * 2 * DH] * DIFF_SCALE; o += H_B * 2 * DH
    w_dk = w_in[:, o:o + H_B * 2 * DH]; o += H_B * 2 * DH
    w_dv = w_in[:, o:o + H_B * DV_B]; o += H_B * DV_B
    w_g = w_in[:, o:]
    pad = HEAD_PAD - DN - DR
    w_qc = jnp.concatenate([qn, qr, jnp.zeros((d, H_A, pad), F32)], -1).reshape(d, H_A * HEAD_PAD)
    w_kr = jnp.concatenate([jnp.zeros((d, ROPE_LO), F32), kr, jnp.zeros((d, pad), F32)], -1)
    w_att = jnp.concatenate([w_qc, w_ckv, w_kr, w_dq, w_dk, w_dv], 1).astype(BF16)
    wuk_pad = jnp.concatenate([w_uk, jnp.zeros((KV_LORA, H_A, HEAD_PAD - DN), F32)], -1)
    wuk_pad = wuk_pad.reshape(KV_LORA, H_A * HEAD_PAD).astype(BF16)
    wuv2d = w_uv.reshape(KV_LORA, H_A * DV_A).astype(BF16)
    return w_att, w_g.astype(BF16), wuk_pad, wuv2d


def kernel(x_prompt, x_sample, cache_mla_latent, cache_mla_krope, cache_diff_k, cache_diff_v,
           page_table, p_prompt, p_sample, rel_bias_table, w_in, kv_norm_g, w_uk, w_uv,
           lam_q1, lam_k1, lam_q2, lam_k2, subln_g, w_a, w_b, w_out, ln1_g, ln1_b,
           w_up, w_down, ln2_g, ln2_b, w_p, w_pg):
    batch, seq, _ = x_prompt.shape
    dbatch, dseq, _ = x_sample.shape
    depth = w_in.shape[0]
    n_pages = page_table.shape[1]
    page = cache_mla_latent.shape[2]
    past = n_pages * page
    alpha = DEPTH_ALPHA(depth)

    tq = 256
    tm_p = 512
    tm_s = min(256, dbatch * dseq)
    n_pg = 8
    ck = n_pg * page
    assert seq % tq == 0 and seq % tm_p == 0 and tm_s % dseq == 0 and n_pages % n_pg == 0
    assert tq > MAX_DIST and ck >= MAX_DIST

    tabs_p = _rope_tables(jnp.arange(seq))
    tabs_s = tuple(jnp.tile(t, (tm_s // dseq, 1)) for t in _rope_tables(past + jnp.arange(dseq)))
    qi = jnp.arange(tq)[:, None]
    ki = jnp.arange(tq)[None, :]
    bkt_p = jnp.stack([_t5_bucket(jnp.maximum(qi - ki + d * tq, 0)) for d in (2, 1, 0)]).astype(I32)
    rows_d = 2 * dseq * H_B
    q_of_row = (jnp.arange(rows_d) % (dseq * H_B)) // H_B
    h_of_row = jnp.arange(rows_d) % H_B
    kc_idx = jnp.arange(ck)[None, :]
    bkt_far = _t5_bucket(past + q_of_row[:, None] - kc_idx)
    bkt_last = _t5_bucket(ck + q_of_row[:, None] - kc_idx)
    bktp = jnp.stack([bkt_far, bkt_last]).astype(I32)
    bktn = _t5_bucket(jnp.maximum(q_of_row[:, None] - jnp.arange(NEW_PAD)[None, :], 0)).astype(I32)
    tcol = rel_bias_table[:, h_of_row][:, :, None]
    sel = jnp.zeros((H_A * HEAD_PAD, LANES), F32)
    rope_rows = (jnp.arange(H_A)[:, None] * HEAD_PAD + ROPE_LO + jnp.arange(DR)[None, :]).reshape(-1)
    sel = sel.at[rope_rows, jnp.tile(jnp.arange(DR), H_A)].set(1.0).astype(BF16)

    n_pool = cache_mla_latent.shape[1]
    caches = (cache_mla_latent, jnp.transpose(cache_mla_krope, (0, 1, 3, 2)),
              jnp.transpose(cache_diff_k, (0, 1, 3, 4, 2)).reshape(depth, n_pool, H_B * 2 * DH, page),
              jnp.transpose(cache_diff_v, (0, 1, 3, 4, 2)).reshape(depth, n_pool, H_B * DV_B, page))

    xp = x_prompt.reshape(batch * seq, D_MODEL)
    xs = x_sample.reshape(dbatch * dseq, D_MODEL)
    outs = [[] for _ in range(8)]
    row2 = lambda a: a.reshape(1, -1)
    for l in range(depth):
        lam_init = 0.8 - 0.6 * math.exp(-0.3 * l)
        w_att, w_g, wuk_pad, wuv2d = _layer_weights(w_in[l], w_uk[l], w_uv[l])
        kvg = row2(kv_norm_g[l])
        lams = [row2(a[l]) for a in (lam_q1, lam_k1, lam_q2, lam_k2)]
        g_head = row2(subln_g[l])
        g_tiled = jnp.tile(g_head, (1, H_B))
        wa, wb, wo = w_a[l].astype(BF16), w_b[l].astype(BF16), w_out[l].astype(BF16)
        wu, wd = w_up[l].astype(BF16), w_down[l].astype(BF16)
        wpg, wp = w_pg[l].astype(BF16), w_p[l].astype(BF16)
        ln = [row2(a[l]) for a in (ln1_g, ln1_b, ln2_g, ln2_b)]

        qc, dq, ckv, kr, dk, dv, kc, vm, dkb, dvb = _proj(xp, w_att, tabs_p, kvg, wuk_pad, wuv2d, tm_p)
        oa, ob = _prompt_attention(rel_bias_table, bkt_p, qc, dq, kc, vm, dkb, dvb, lams, g_head,
                                   lam_init, batch, seq, tq)
        for lst, a in zip(outs[0::2], (ckv, kr, dk, dv)):
            lst.append(a)
        x1 = _merge(xp, oa, ob, w_g, wa, wb, wo, ln[0], ln[1], alpha, tm_p)
        xp = _ffn(x1, p_prompt[l].reshape(batch * seq, -1), wu, wd, ln[2], ln[3], wpg, wp, alpha, tm_p)

        qc, dq, ckv, kr, dk, dv, _, _, _, _ = _proj(xs, w_att, tabs_s, kvg, wuk_pad, wuv2d, tm_s)
        oa, ob = _sample_attention(l, page_table, caches, qc, dq, ckv, kr, dk, dv, wuk_pad, sel, wuv2d,
                                   tcol, bktp, bktn, lams, g_tiled, lam_init, n_pg)
        for lst, a in zip(outs[1::2], (ckv, kr, dk, dv)):
            lst.append(a)
        x1 = _merge(xs, oa, ob, w_g, wa, wb, wo, ln[0], ln[1], alpha, tm_s)
        xs = _ffn(x1, p_sample[l].reshape(dbatch * dseq, -1), wu, wd, ln[2], ln[3], wpg, wp, alpha, tm_s)

    def stack(lst, lead, tail):
        return jnp.stack(lst).reshape(depth, *lead, *tail)

    lp, ls = (batch, seq), (dbatch, dseq)
    return (xp.reshape(batch, seq, D_MODEL), xs.reshape(dbatch, dseq, D_MODEL),
            stack(outs[0], lp, (KV_LORA,)), stack(outs[1], ls, (KV_LORA,)),
            stack(outs[2], lp, (DR,)), stack(outs[3], ls, (DR,)),
            stack(outs[4], lp, (H_B, 2 * DH)), stack(outs[5], ls, (H_B, 2 * DH)),
            stack(outs[6], lp, (H_B, DV_B)), stack(outs[7], ls, (H_B, DV_B)))
```

```python
import functools
import math

import jax
import jax.numpy as jnp
from jax import lax
from jax.experimental import pallas as pl
from jax.experimental.pallas import tpu as pltpu

F32 = jnp.float32
BF16 = jnp.bfloat16
I32 = jnp.int32

D_MODEL = 1024
H_A, DN, DR, DV_A, KV_LORA = 8, 64, 32, 64, 256
H_B, DH, DV_B = 8, 32, 64
MLA_SCALE = (DN + DR) ** -0.5
DIFF_SCALE = DH ** -0.5
ROPE_THETA = 10000.0
N_BUCKETS, MAX_DIST = 32, 128
D_FF = 4 * D_MODEL
DEPTH_ALPHA = lambda depth: (2 * depth) ** 0.25
EPS = 1e-5
NEG = -1e30

LANES = 128
HEAD_PAD = LANES
ROPE_LO = DN
VMEM_LIMIT = 56 * 1024 * 1024

QC0, QC1 = 0, H_A * HEAD_PAD
CKV0, CKV1 = QC1, QC1 + KV_LORA
KR0, KR1 = CKV1, CKV1 + HEAD_PAD
DQ0, DQ1 = KR1, KR1 + H_B * 2 * DH
DK0, DK1 = DQ1, DQ1 + H_B * 2 * DH
DV0, DV1 = DK1, DK1 + H_B * DV_B
N_ATT = DV1


def _dot(a, b):
    return jnp.dot(a, b, preferred_element_type=F32)


def _dot_nt(a, b):
    return lax.dot_general(a, b, (((1,), (1,)), ((), ())), preferred_element_type=F32)


def _const_spec(shape):
    nd = len(shape)
    return pl.BlockSpec(shape, lambda *_: (0,) * nd)


def _params(sem):
    return pltpu.CompilerParams(dimension_semantics=sem, vmem_limit_bytes=VMEM_LIMIT)


def _rope128(x, c, sa, sb):
    return x * c + pltpu.roll(x, LANES - DR // 2, 1) * sa + pltpu.roll(x, DR // 2, 1) * sb


def _store_head_padded(out_ref, z, fill):
    rows = z.shape[0]
    low = lax.broadcasted_iota(I32, (rows, LANES), 1) < DV_B
    for g in range(z.shape[1] // LANES):
        pair = z[:, g * LANES:(g + 1) * LANES]
        out_ref[:, (2 * g) * LANES:(2 * g + 1) * LANES] = jnp.where(low, pair, fill).astype(BF16)
        swapped = pltpu.roll(pair, LANES // 2, 1)
        out_ref[:, (2 * g + 1) * LANES:(2 * g + 2) * LANES] = jnp.where(low, swapped, fill).astype(BF16)


def _proj_body(x_ref, w_ref, c_ref, sa_ref, sb_ref, kvg_ref, wuk_ref, wuv_ref,
               qc_ref, dq_ref, ckv_ref, kr_ref, dk_ref, dv_ref, kc_ref, vm_ref, dqp_ref, dkp_ref, dvp_ref):
    xb = x_ref[...].astype(BF16)
    c, sa, sb = c_ref[...], sa_ref[...], sb_ref[...]
    zq = _dot(xb, w_ref[:, QC0:QC1])
    for h in range(H_A):
        sl = slice(h * HEAD_PAD, (h + 1) * HEAD_PAD)
        qc_ref[:, sl] = _rope128(zq[:, sl], c, sa, sb).astype(BF16)
    zc = _dot(xb, w_ref[:, CKV0:CKV1])
    ckv = zc * lax.rsqrt(jnp.mean(zc * zc, -1, keepdims=True) + EPS) * kvg_ref[...]
    ckv_ref[...] = ckv
    cb = ckv.astype(BF16)
    zk = _rope128(_dot(xb, w_ref[:, KR0:KR1]), c, sa, sb)
    kr_ref[...] = zk[:, ROPE_LO:ROPE_LO + DR]
    kn = _dot(cb, wuk_ref[...])
    for h in range(H_A):
        sl = slice(h * HEAD_PAD, (h + 1) * HEAD_PAD)
        kc_ref[:, sl] = (kn[:, sl] + zk).astype(BF16)
    _store_head_padded(vm_ref, _dot(cb, wuv_ref[...]), 1.0)
    zdq = _dot(xb, w_ref[:, DQ0:DQ1])
    dq_ref[...] = zdq.astype(BF16)
    _store_head_padded(dqp_ref, zdq, 0.0)
    zdk = _dot(xb, w_ref[:, DK0:DK1])
    dk_ref[...] = zdk
    _store_head_padded(dkp_ref, zdk, 0.0)
    zdv = _dot(xb, w_ref[:, DV0:DV1])
    dv_ref[...] = zdv
    _store_head_padded(dvp_ref, zdv, 1.0)


def _proj(x, w_att, tabs, kvg, wuk_pad, wuv2d, tm):
    t = x.shape[0]
    c, sa, sb = tabs
    tab_blocks = c.shape[0] // tm
    row = lambda w: pl.BlockSpec((tm, w), lambda i: (i, 0))
    tab = pl.BlockSpec((tm, LANES), lambda i: (i % tab_blocks, 0))
    out_shapes = [
        jax.ShapeDtypeStruct((t, H_A * HEAD_PAD), BF16),
        jax.ShapeDtypeStruct((t, H_B * 2 * DH), BF16),
        jax.ShapeDtypeStruct((t, KV_LORA), F32),
        jax.ShapeDtypeStruct((t, DR), F32),
        jax.ShapeDtypeStruct((t, H_B * 2 * DH), F32),
        jax.ShapeDtypeStruct((t, H_B * DV_B), F32),
        jax.ShapeDtypeStruct((t, H_A * HEAD_PAD), BF16),
        jax.ShapeDtypeStruct((t, H_A * HEAD_PAD), BF16),
        jax.ShapeDtypeStruct((t, H_B * HEAD_PAD), BF16),
        jax.ShapeDtypeStruct((t, H_B * HEAD_PAD), BF16),
        jax.ShapeDtypeStruct((t, H_B * HEAD_PAD), BF16),
    ]
    return pl.pallas_call(
        _proj_body,
        grid=(t // tm,),
        in_specs=[row(D_MODEL), _const_spec(w_att.shape), tab, tab, tab,
                  _const_spec(kvg.shape), _const_spec(wuk_pad.shape), _const_spec(wuv2d.shape)],
        out_specs=[row(s.shape[1]) for s in out_shapes],
        out_shape=out_shapes,
        compiler_params=_params(("arbitrary",)),
        name="proj",
    )(x, w_att, c, sa, sb, kvg, wuk_pad, wuv2d)


def _lam_value(lq1, lk1, lq2, lk2, lam_init):
    a = jnp.exp(jnp.sum(lq1[...] * lk1[...], -1, keepdims=True))
    b = jnp.exp(jnp.sum(lq2[...] * lk2[...], -1, keepdims=True))
    return a - b + lam_init


def _online_aug(m_ref, acc_ref, idx, s, v_aug):
    m_prev = m_ref[idx]
    m_new = jnp.maximum(m_prev, jnp.max(s, -1, keepdims=True))
    alpha = jnp.exp(m_prev - m_new)
    p = jnp.exp(s - pltpu.repeat(m_new, s.shape[1] // LANES, 1))
    acc_ref[idx] = alpha * acc_ref[idx] + _dot(p.astype(BF16), v_aug)
    m_ref[idx] = m_new


def _online(m_ref, l_ref, acc_ref, rows, arows, s, v, v_transposed=False):
    m_prev = m_ref[rows]
    m_new = jnp.maximum(m_prev, jnp.max(s, -1, keepdims=True))
    alpha = jnp.exp(m_prev - m_new)
    p = jnp.exp(s - m_new)
    l_ref[rows] = alpha * l_ref[rows] + jnp.sum(p, -1, keepdims=True)
    pv = _dot_nt(p.astype(BF16), v) if v_transposed else _dot(p.astype(BF16), v)
    acc_ref[arows] = alpha * acc_ref[arows] + pv
    m_ref[rows] = m_new


def _attn_body(table_ref, bkt_ref, qc_ref, dq_ref, kc_ref, vm_ref, dk_ref, dv_ref,
               lq1, lk1, lq2, lk2, g_ref, oa_ref, ob_ref,
               bias_ref, qd_ref, m_ref, acc_ref, *, lam_init, tq):
    b = pl.program_id(0)
    i = pl.program_id(1)

    @pl.when((b == 0) & (i == 0))
    def _build_bias():
        bias_ref[...] = jnp.zeros(bias_ref.shape, F32)

        def per_bucket(n, carry):
            for t in range(3):
                hit = bkt_ref[t] == n
                for h in range(H_B):
                    bias_ref[t, h] = jnp.where(hit, table_ref[n, h], bias_ref[t, h])
            return carry

        lax.fori_loop(0, N_BUCKETS, per_bucket, 0)
        causal = (lax.broadcasted_iota(I32, (tq, tq), 0) >= lax.broadcasted_iota(I32, (tq, tq), 1))
        for h in range(H_B):
            bias_ref[2, h] = jnp.where(causal, bias_ref[2, h], NEG)
        bias_ref[2, H_B] = jnp.where(causal, 0.0, NEG)

    m_ref[...] = jnp.full(m_ref.shape, NEG, F32)
    acc_ref[...] = jnp.zeros(acc_ref.shape, F32)

    lane = lax.broadcasted_iota(I32, (tq, HEAD_PAD), 1)
    for h in range(H_B):
        qh = dq_ref[:, h * HEAD_PAD:(h + 1) * HEAD_PAD]
        zero = jnp.zeros_like(qh)
        qd_ref[2 * h] = jnp.where(lane < DH, qh, zero)
        qd_ref[2 * h + 1] = jnp.where(lane >= DH, qh, zero)

    def step(j, carry):
        t = 2 - jnp.minimum(i - j, 2)
        ks = pl.ds(pl.multiple_of(j * tq, tq), tq)
        mask = bias_ref[t, H_B]
        for h in range(H_A):
            hs = slice(h * HEAD_PAD, (h + 1) * HEAD_PAD)
            s = _dot_nt(qc_ref[:, hs], kc_ref[ks, hs]) + mask
            _online_aug(m_ref, acc_ref, h, s, vm_ref[ks, hs])
        for h in range(H_B):
            hs = slice(h * HEAD_PAD, (h + 1) * HEAD_PAD)
            kh = dk_ref[ks, hs]
            vh = dv_ref[ks, hs]
            bias = bias_ref[t, h]
            for jm in range(2):
                s = _dot_nt(qd_ref[2 * h + jm], kh) + bias
                _online_aug(m_ref, acc_ref, H_A + 2 * h + jm, s, vh)
        return carry

    lax.fori_loop(0, i + 1, step, 0)

    low = lane < DV_B

    def normalised(idx):
        a = acc_ref[idx]
        return a * pltpu.roll(1.0 / a, LANES // 2, 1)

    def store_pairs(out_ref, heads):
        for g in range(len(heads) // 2):
            pair = jnp.where(low, heads[2 * g], pltpu.roll(heads[2 * g + 1], LANES // 2, 1))
            out_ref[:, g * LANES:(g + 1) * LANES] = pair.astype(out_ref.dtype)

    lam = _lam_value(lq1, lk1, lq2, lk2, lam_init)
    g = g_ref[...]
    store_pairs(oa_ref, [normalised(h) for h in range(H_A)])
    diff = []
    for h in range(H_B):
        o = jnp.where(low, normalised(H_A + 2 * h) - lam * normalised(H_A + 2 * h + 1), 0.0)
        ms = jnp.sum(o * o, -1, keepdims=True) * (1.0 / DV_B)
        diff.append(o * lax.rsqrt(ms + EPS) * g * (1.0 - lam_init))
    store_pairs(ob_ref, diff)


def _prompt_attention(table, bkt, qc, dq, kc, vm, dkb, dvb, lams, g, lam_init, batch, seq, tq):
    r3 = lambda a: a.reshape(batch, seq, a.shape[-1])
    qc, dq, kc, vm, dkb, dvb = map(r3, (qc, dq, kc, vm, dkb, dvb))
    qblk = lambda w: pl.BlockSpec((None, tq, w), lambda b, i: (b, i, 0))
    full = lambda w: pl.BlockSpec((None, seq, w), lambda b, i: (b, 0, 0), pipeline_mode=pl.Buffered(1))
    small = [_const_spec(a.shape) for a in lams] + [_const_spec(g.shape)]
    oa, ob = pl.pallas_call(
        functools.partial(_attn_body, lam_init=lam_init, tq=tq),
        grid=(batch, seq // tq),
        in_specs=[pl.BlockSpec(memory_space=pltpu.SMEM), _const_spec(bkt.shape),
                  qblk(qc.shape[-1]), qblk(dq.shape[-1]),
                  full(kc.shape[-1]), full(vm.shape[-1]), full(dkb.shape[-1]), full(dvb.shape[-1])] + small,
        out_specs=[qblk(H_A * DV_A), qblk(H_B * DV_B)],
        out_shape=[jax.ShapeDtypeStruct((batch, seq, H_A * DV_A), BF16),
                   jax.ShapeDtypeStruct((batch, seq, H_B * DV_B), BF16)],
        scratch_shapes=[pltpu.VMEM((3, H_B + 1, tq, tq), F32),
                        pltpu.VMEM((2 * H_B, tq, HEAD_PAD), BF16),
                        pltpu.VMEM((H_A + 2 * H_B, tq, LANES), F32),
                        pltpu.VMEM((H_A + 2 * H_B, tq, HEAD_PAD), F32)],
        compiler_params=_params(("arbitrary", "arbitrary")),
        name="prompt_attn",
    )(table, bkt, qc, dq, kc, vm, dkb, dvb, *lams, g)
    return oa.reshape(batch * seq, -1), ob.reshape(batch * seq, -1)


NEW_PAD = LANES


def _dec_body(pt_ref, qc_ref, dq_ref, ckvn_ref, krn_ref, dkn_ref, dvn_ref,
              wuk_ref, sel_ref, wuv_ref, tcol_ref, bktp_ref, bktn_ref,
              lq1, lk1, lq2, lk2, g_ref, *rest, lam_init, n_pg, page, n_q):
    lat_refs, kr_refs = rest[0:n_pg], rest[n_pg:2 * n_pg]
    k_refs, v_refs = rest[2 * n_pg:3 * n_pg], rest[3 * n_pg:4 * n_pg]
    oa_ref, ob_ref = rest[4 * n_pg:4 * n_pg + 2]
    (bias_ref, biasn_ref, kb_ref, krt_ref, kt_ref, vt_ref, nb_ref, nk_ref, nv_ref,
     qbd_ref, qm_ref, qd_ref, m_ref, l_ref, acca_ref, accd_ref) = rest[4 * n_pg + 2:]
    del pt_ref
    b = pl.program_id(0)
    c = pl.program_id(1)
    last = pl.num_programs(1) - 1
    rows_a = n_q * H_A
    rows_d = 2 * n_q * H_B

    @pl.when((b == 0) & (c == 0))
    def _once():
        bias_ref[...] = jnp.zeros(bias_ref.shape, F32)
        biasn_ref[...] = jnp.zeros(biasn_ref.shape, F32)

        def per_bucket(n, carry):
            col = tcol_ref[n]
            for t in range(2):
                bias_ref[t] = jnp.where(bktp_ref[t] == n, col, bias_ref[t])
            biasn_ref[...] = jnp.where(bktn_ref[...] == n, col, biasn_ref[...])
            return carry

        lax.fori_loop(0, N_BUCKETS, per_bucket, 0)

    @pl.when(c == 0)
    def _per_sequence():
        m_ref[...] = jnp.full(m_ref.shape, NEG, F32)
        l_ref[...] = jnp.zeros(l_ref.shape, F32)
        acca_ref[...] = jnp.zeros(acca_ref.shape, F32)
        accd_ref[...] = jnp.zeros(accd_ref.shape, F32)
        head_a = lax.broadcasted_iota(I32, (H_A, H_A * HEAD_PAD), 1) // HEAD_PAD
        row_a = lax.broadcasted_iota(I32, (H_A, H_A * HEAD_PAD), 0)
        lane_d = lax.broadcasted_iota(I32, (H_B, H_B * 2 * DH), 1)
        row_d = lax.broadcasted_iota(I32, (H_B, H_B * 2 * DH), 0)
        qcf = qc_ref[...].astype(F32)
        dqf = dq_ref[...].astype(F32)
        for q in range(n_q):
            qrow = jnp.broadcast_to(qcf[q:q + 1, :], (H_A, H_A * HEAD_PAD))
            qbd_ref[q * H_A:(q + 1) * H_A, :] = jnp.where(head_a == row_a, qrow, 0.0).astype(BF16)
            drow = jnp.broadcast_to(dqf[q:q + 1, :], (H_B, H_B * 2 * DH))
            for jm in range(2):
                keep = (lane_d // (2 * DH) == row_d) & ((lane_d % (2 * DH)) // DH == jm)
                r0 = jm * n_q * H_B + q * H_B
                qd_ref[r0:r0 + H_B, :] = jnp.where(keep, drow, 0.0).astype(BF16)
        qbd = qbd_ref[...]
        qm_ref[:, 0:KV_LORA] = _dot_nt(qbd, wuk_ref[...]).astype(BF16)
        qm_ref[:, KV_LORA:] = _dot(qbd, sel_ref[...]).astype(BF16)

    for p in range(n_pg):
        rs = slice(p * page, (p + 1) * page)
        kb_ref[rs, :] = lat_refs[p][...].astype(BF16)
        krt_ref[:, rs] = kr_refs[p][...].astype(BF16)
        kt_ref[:, rs] = k_refs[p][...].astype(BF16)
        vt_ref[:, rs] = v_refs[p][...].astype(BF16)

    ra = pl.ds(0, rows_a)
    rd = pl.ds(rows_a, rows_d)
    s = (_dot_nt(qm_ref[:, 0:KV_LORA], kb_ref[...])
         + _dot(qm_ref[:, KV_LORA:KV_LORA + DR], krt_ref[...]))
    _online(m_ref, l_ref, acca_ref, ra, slice(None), s, kb_ref[...])
    t = jnp.where(c == last, 1, 0)
    sd = _dot(qd_ref[...], kt_ref[...]) + bias_ref[t]
    _online(m_ref, l_ref, accd_ref, rd, slice(None), sd, vt_ref[...], v_transposed=True)

    @pl.when(c == last)
    def _finish():
        nb_ref[...] = jnp.zeros(nb_ref.shape, F32)
        nk_ref[...] = jnp.zeros(nk_ref.shape, F32)
        nv_ref[...] = jnp.zeros(nv_ref.shape, F32)
        nb_ref[0:n_q, 0:KV_LORA] = ckvn_ref[...]
        nb_ref[0:n_q, KV_LORA:KV_LORA + DR] = krn_ref[...]
        nk_ref[0:n_q, :] = dkn_ref[...]
        nv_ref[0:n_q, :] = dvn_ref[...]
        nb = nb_ref[...].astype(BF16)
        key = lax.broadcasted_iota(I32, (rows_a, NEW_PAD), 1)
        qry = lax.broadcasted_iota(I32, (rows_a, NEW_PAD), 0) // H_A
        s = jnp.where(key <= qry, _dot_nt(qm_ref[...], nb), NEG)
        _online(m_ref, l_ref, acca_ref, ra, slice(None), s, nb[:, 0:KV_LORA])
        key = lax.broadcasted_iota(I32, (rows_d, NEW_PAD), 1)
        qry = (lax.broadcasted_iota(I32, (rows_d, NEW_PAD), 0) % (n_q * H_B)) // H_B
        sd = _dot_nt(qd_ref[...], nk_ref[...].astype(BF16)) + biasn_ref[...]
        sd = jnp.where(key <= qry, sd, NEG)
        _online(m_ref, l_ref, accd_ref, rd, slice(None), sd, nv_ref[...].astype(BF16))

        o_lat = (acca_ref[...] / l_ref[ra]).astype(BF16)
        ra_full = _dot(o_lat, wuv_ref[...])
        own_a = (lax.broadcasted_iota(I32, (rows_a, H_A * DV_A), 1) // DV_A
                 == lax.broadcasted_iota(I32, (rows_a, H_A * DV_A), 0) % H_A)
        ra_full = jnp.where(own_a, ra_full, 0.0)
        lam = _lam_value(lq1, lk1, lq2, lk2, lam_init)
        half = n_q * H_B
        o1 = accd_ref[0:half, :] / l_ref[pl.ds(rows_a, half)]
        o2 = accd_ref[half:2 * half, :] / l_ref[pl.ds(rows_a + half, half)]
        own_d = (lax.broadcasted_iota(I32, (half, H_B * DV_B), 1) // DV_B
                 == lax.broadcasted_iota(I32, (half, H_B * DV_B), 0) % H_B)
        o = jnp.where(own_d, o1 - lam * o2, 0.0)
        ms = jnp.sum(o * o, -1, keepdims=True) * (1.0 / DV_B)
        o = o * lax.rsqrt(ms + EPS) * g_ref[...] * (1.0 - lam_init)
        for q in range(n_q):
            oa_ref[q:q + 1, :] = jnp.sum(ra_full[q * H_A:(q + 1) * H_A, :], 0, keepdims=True)
            ob_ref[q:q + 1, :] = jnp.sum(o[q * H_B:(q + 1) * H_B, :], 0, keepdims=True)


def _sample_attention(layer, page_table, caches, qc, dq, ckv, kr, dk, dv, wuk_pad, sel, wuv2d,
                      tcol, bktp, bktn, lams, g_tiled, lam_init, n_pg):
    lat_pool, kr_pool, k_pool, v_pool = caches
    bsz, n_pages = page_table.shape
    n_q = qc.shape[0] // bsz
    page = lat_pool.shape[2]
    assert page == LANES
    ck = n_pg * page
    r3 = lambda a: a.reshape(bsz, n_q, a.shape[-1])
    per_seq = [r3(a) for a in (qc, dq, ckv, kr, dk, dv)]
    seq_spec = lambda a: pl.BlockSpec((None, n_q, a.shape[-1]), lambda b, c, pt: (b, 0, 0))
    cst = lambda a: pl.BlockSpec(a.shape, lambda b, c, pt, nd=a.ndim: (0,) * nd)

    def page_specs(pool):
        r, w = pool.shape[2:]
        return [pl.BlockSpec((None, None, r, w),
                             lambda b, c, pt, p=p: (layer, pt[b, c * n_pg + p], 0, 0))
                for p in range(n_pg)]

    consts = [wuk_pad, sel, wuv2d, tcol, bktp, bktn, *lams, g_tiled]
    rows_a, rows_d = n_q * H_A, 2 * n_q * H_B
    lat_w = KV_LORA + LANES
    kw = k_pool.shape[2]
    grid_spec = pltpu.PrefetchScalarGridSpec(
        num_scalar_prefetch=1,
        grid=(bsz, n_pages // n_pg),
        in_specs=[seq_spec(a) for a in per_seq] + [cst(a) for a in consts]
        + page_specs(lat_pool) + page_specs(kr_pool) + page_specs(k_pool) + page_specs(v_pool),
        out_specs=[pl.BlockSpec((None, n_q, H_A * DV_A), lambda b, c, pt: (b, 0, 0)),
                   pl.BlockSpec((None, n_q, H_B * DV_B), lambda b, c, pt: (b, 0, 0))],
        scratch_shapes=[pltpu.VMEM((2, rows_d, ck), F32),
                        pltpu.VMEM((rows_d, NEW_PAD), F32),
                        pltpu.VMEM((ck, KV_LORA), BF16),
                        pltpu.VMEM((DR, ck), BF16),
                        pltpu.VMEM((kw, ck), BF16),
                        pltpu.VMEM((kw, ck), BF16),
                        pltpu.VMEM((NEW_PAD, lat_w), F32),
                        pltpu.VMEM((NEW_PAD, kw), F32),
                        pltpu.VMEM((NEW_PAD, kw), F32),
                        pltpu.VMEM((rows_a, H_A * HEAD_PAD), BF16),
                        pltpu.VMEM((rows_a, lat_w), BF16),
                        pltpu.VMEM((rows_d, kw), BF16),
                        pltpu.VMEM((rows_a + rows_d, 1), F32),
                        pltpu.VMEM((rows_a + rows_d, 1), F32),
                        pltpu.VMEM((rows_a, KV_LORA), F32),
                        pltpu.VMEM((rows_d, kw), F32)],
    )
    oa, ob = pl.pallas_call(
        functools.partial(_dec_body, lam_init=lam_init, n_pg=n_pg, page=page, n_q=n_q),
        grid_spec=grid_spec,
        out_shape=[jax.ShapeDtypeStruct((bsz, n_q, H_A * DV_A), F32),
                   jax.ShapeDtypeStruct((bsz, n_q, H_B * DV_B), F32)],
        compiler_params=_params(("arbitrary", "arbitrary")),
        name="sample_attn",
    )(page_table, *per_seq, *consts,
      *([lat_pool] * n_pg), *([kr_pool] * n_pg), *([k_pool] * n_pg), *([v_pool] * n_pg))
    return oa.reshape(bsz * n_q, -1), ob.reshape(bsz * n_q, -1)


def _layer_norm(y, g, b):
    yc = y - jnp.mean(y, -1, keepdims=True)
    var = jnp.mean(yc * yc, -1, keepdims=True)
    return yc * lax.rsqrt(var + EPS) * g + b


def _merge_body(x_ref, oa_ref, ob_ref, wg_ref, wa_ref, wb_ref, wo_ref, g_ref, b_ref, y_ref, *, alpha):
    x = x_ref[...]
    xb = x.astype(BF16)
    ga = jax.nn.sigmoid(_dot(xb, wg_ref[:, 0:D_MODEL]))
    gb = jax.nn.sigmoid(_dot(xb, wg_ref[:, D_MODEL:2 * D_MODEL]))
    mix = (ga * _dot(oa_ref[...].astype(BF16), wa_ref[...])
           + gb * _dot(ob_ref[...].astype(BF16), wb_ref[...]))
    y = alpha * x + _dot(mix.astype(BF16), wo_ref[...])
    y_ref[...] = _layer_norm(y, g_ref[...], b_ref[...])


def _merge(x, oa, ob, wg, wa, wb, wo, g, b, alpha, tm):
    t = x.shape[0]
    row = lambda a: pl.BlockSpec((tm, a.shape[1]), lambda i: (i, 0))
    ws = [wg, wa, wb, wo, g, b]
    return pl.pallas_call(
        functools.partial(_merge_body, alpha=alpha),
        grid=(t // tm,),
        in_specs=[row(x), row(oa), row(ob)] + [_const_spec(w.shape) for w in ws],
        out_specs=row(x),
        out_shape=jax.ShapeDtypeStruct(x.shape, F32),
        compiler_params=_params(("arbitrary",)),
        name="merge",
    )(x, oa, ob, *ws)


FF_CHUNK = 512


def _ffn_body(x_ref, p_ref, wu_ref, wd_ref, g_ref, b_ref, wpg_ref, wp_ref, y_ref, *, alpha):
    x = x_ref[...]
    xb = x.astype(BF16)
    h = jnp.zeros(x.shape, F32)
    for k in range(D_FF // FF_CHUNK):
        sl = slice(k * FF_CHUNK, (k + 1) * FF_CHUNK)
        u = jnp.maximum(_dot(xb, wu_ref[:, sl]), 0.0)
        h = h + _dot((u * u).astype(BF16), wd_ref[sl, :])
    y = _layer_norm(alpha * x + h, g_ref[...], b_ref[...])
    gate = jax.nn.sigmoid(_dot(y.astype(BF16), wpg_ref[...]))
    y_ref[...] = y + gate * _dot(p_ref[...].astype(BF16), wp_ref[...])


def _ffn(x, p, wu, wd, g, b, wpg, wp, alpha, tm):
    t = x.shape[0]
    row = lambda a: pl.BlockSpec((tm, a.shape[1]), lambda i: (i, 0))
    ws = [wu, wd, g, b, wpg, wp]
    return pl.pallas_call(
        functools.partial(_ffn_body, alpha=alpha),
        grid=(t // tm,),
        in_specs=[row(x), row(p)] + [_const_spec(w.shape) for w in ws],
        out_specs=row(x),
        out_shape=jax.ShapeDtypeStruct(x.shape, F32),
        compiler_params=_params(("arbitrary",)),
        name="ffn",
    )(x, p, *ws)


def _t5_bucket(n):
    exact = N_BUCKETS // 2
    nf = jnp.maximum(n, 1).astype(F32)
    large = exact + (jnp.log(nf / exact) / math.log(MAX_DIST / exact) * (N_BUCKETS - exact)).astype(I32)
    large = jnp.minimum(large, N_BUCKETS - 1)
    return jnp.where(n < exact, n, large)


def _rope_tables(pos):
    half = DR // 2
    inv = ROPE_THETA ** (-jnp.arange(half, dtype=F32) / half)
    ang = pos.astype(F32)[:, None] * inv[None, :]
    cos, sin = jnp.cos(ang), jnp.sin(ang)
    n = pos.shape[0]
    ones = jnp.ones((n, ROPE_LO), F32)
    z = lambda w: jnp.zeros((n, w), F32)
    pad = HEAD_PAD - ROPE_LO - DR
    c = jnp.concatenate([ones, cos, cos, z(pad)], 1)
    sa = jnp.concatenate([z(ROPE_LO), -sin, z(half), z(pad)], 1)
    sb = jnp.concatenate([z(ROPE_LO), z(half), sin, z(pad)], 1)
    return c, sa, sb


def _layer_weights(w_in, w_uk, w_uv):
    d = w_in.shape[0]
    o = 0
    qn = w_in[:, o:o + H_A * DN].reshape(d, H_A, DN) * MLA_SCALE; o += H_A * DN
    qr = w_in[:, o:o + H_A * DR].reshape(d, H_A, DR) * MLA_SCALE; o += H_A * DR
    w_ckv = w_in[:, o:o + KV_LORA]; o += KV_LORA
    kr = w_in[:, o:o + DR]; o += DR
    w_dq = w_in[:, o:o + H_B * 2 * DH] * DIFF_SCALE; o += H_B * 2 * DH
    w_dk = w_in[:, o:o + H_B * 2 * DH]; o += H_B * 2 * DH
    w_dv = w_in[:, o:o + H_B * DV_B]; o += H_B * DV_B
    w_g = w_in[:, o:]
    pad = HEAD_PAD - DN - DR
    w_qc = jnp.concatenate([qn, qr, jnp.zeros((d, H_A, pad), F32)], -1).reshape(d, H_A * HEAD_PAD)
    w_kr = jnp.concatenate([jnp.zeros((d, ROPE_LO), F32), kr, jnp.zeros((d, pad), F32)], -1)
    w_att = jnp.concatenate([w_qc, w_ckv, w_kr, w_dq, w_dk, w_dv], 1).astype(BF16)
    wuk_pad = jnp.concatenate([w_uk, jnp.zeros((KV_LORA, H_A, HEAD_PAD - DN), F32)], -1)
    wuk_pad = wuk_pad.reshape(KV_LORA, H_A * HEAD_PAD).astype(BF16)
    wuv2d = w_uv.reshape(KV_LORA, H_A * DV_A).astype(BF16)
    return w_att, w_g.astype(BF16), wuk_pad, wuv2d


def kernel(x_prompt, x_sample, cache_mla_latent, cache_mla_krope, cache_diff_k, cache_diff_v,
           page_table, p_prompt, p_sample, rel_bias_table, w_in, kv_norm_g, w_uk, w_uv,
           lam_q1, lam_k1, lam_q2, lam_k2, subln_g, w_a, w_b, w_out, ln1_g, ln1_b,
           w_up, w_down, ln2_g, ln2_b, w_p, w_pg):
    batch, seq, _ = x_prompt.shape
    dbatch, dseq, _ = x_sample.shape
    depth = w_in.shape[0]
    n_pages = page_table.shape[1]
    page = cache_mla_latent.shape[2]
    past = n_pages * page
    alpha = DEPTH_ALPHA(depth)

    tq = 256
    tm_p = 512
    tm_s = min(256, dbatch * dseq)
    n_pg = 16
    ck = n_pg * page
    assert seq % tq == 0 and seq % tm_p == 0 and tm_s % dseq == 0 and n_pages % n_pg == 0
    assert tq > MAX_DIST and ck >= MAX_DIST

    tabs_p = _rope_tables(jnp.arange(seq))
    tabs_s = tuple(jnp.tile(t, (tm_s // dseq, 1)) for t in _rope_tables(past + jnp.arange(dseq)))
    qi = jnp.arange(tq)[:, None]
    ki = jnp.arange(tq)[None, :]
    bkt_p = jnp.stack([_t5_bucket(jnp.maximum(qi - ki + d * tq, 0)) for d in (2, 1, 0)]).astype(I32)
    rows_d = 2 * dseq * H_B
    q_of_row = (jnp.arange(rows_d) % (dseq * H_B)) // H_B
    h_of_row = jnp.arange(rows_d) % H_B
    kc_idx = jnp.arange(ck)[None, :]
    bkt_far = _t5_bucket(past + q_of_row[:, None] - kc_idx)
    bkt_last = _t5_bucket(ck + q_of_row[:, None] - kc_idx)
    bktp = jnp.stack([bkt_far, bkt_last]).astype(I32)
    bktn = _t5_bucket(jnp.maximum(q_of_row[:, None] - jnp.arange(NEW_PAD)[None, :], 0)).astype(I32)
    tcol = rel_bias_table[:, h_of_row][:, :, None]
    sel = jnp.zeros((H_A * HEAD_PAD, LANES), F32)
    rope_rows = (jnp.arange(H_A)[:, None] * HEAD_PAD + ROPE_LO + jnp.arange(DR)[None, :]).reshape(-1)
    sel = sel.at[rope_rows, jnp.tile(jnp.arange(DR), H_A)].set(1.0).astype(BF16)

    n_pool = cache_mla_latent.shape[1]
    caches = (cache_mla_latent, jnp.transpose(cache_mla_krope, (0, 1, 3, 2)),
              jnp.transpose(cache_diff_k, (0, 1, 3, 4, 2)).reshape(depth, n_pool, H_B * 2 * DH, page),
              jnp.transpose(cache_diff_v, (0, 1, 3, 4, 2)).reshape(depth, n_pool, H_B * DV_B, page))

    xp = x_prompt.reshape(batch * seq, D_MODEL)
    xs = x_sample.reshape(dbatch * dseq, D_MODEL)
    outs = [[] for _ in range(8)]
    row2 = lambda a: a.reshape(1, -1)
    for l in range(depth):
        lam_init = 0.8 - 0.6 * math.exp(-0.3 * l)
        w_att, w_g, wuk_pad, wuv2d = _layer_weights(w_in[l], w_uk[l], w_uv[l])
        kvg = row2(kv_norm_g[l])
        lams = [row2(a[l]) for a in (lam_q1, lam_k1, lam_q2, lam_k2)]
        g_head = row2(subln_g[l])
        g_tiled = jnp.tile(g_head, (1, H_B))
        g_pad = jnp.concatenate([g_head, jnp.zeros((1, HEAD_PAD - DV_B), F32)], 1)
        wa, wb, wo = w_a[l].astype(BF16), w_b[l].astype(BF16), w_out[l].astype(BF16)
        wu, wd = w_up[l].astype(BF16), w_down[l].astype(BF16)
        wpg, wp = w_pg[l].astype(BF16), w_p[l].astype(BF16)
        ln = [row2(a[l]) for a in (ln1_g, ln1_b, ln2_g, ln2_b)]

        qc, _, ckv, kr, dk, dv, kc, vm, dqp, dkp, dvp = _proj(xp, w_att, tabs_p, kvg, wuk_pad, wuv2d, tm_p)
        oa, ob = _prompt_attention(rel_bias_table, bkt_p, qc, dqp, kc, vm, dkp, dvp, lams, g_pad,
                                   lam_init, batch, seq, tq)
        for lst, a in zip(outs[0::2], (ckv, kr, dk, dv)):
            lst.append(a)
        x1 = _merge(xp, oa, ob, w_g, wa, wb, wo, ln[0], ln[1], alpha, tm_p)
        xp = _ffn(x1, p_prompt[l].reshape(batch * seq, -1), wu, wd, ln[2], ln[3], wpg, wp, alpha, tm_p)

        qc, dq, ckv, kr, dk, dv = _proj(xs, w_att, tabs_s, kvg, wuk_pad, wuv2d, tm_s)[:6]
        oa, ob = _sample_attention(l, page_table, caches, qc, dq, ckv, kr, dk, dv, wuk_pad, sel, wuv2d,
                                   tcol, bktp, bktn, lams, g_tiled, lam_init, n_pg)
        for lst, a in zip(outs[1::2], (ckv, kr, dk, dv)):
            lst.append(a)
        x1 = _merge(xs, oa, ob, w_g, wa, wb, wo, ln[0], ln[1], alpha, tm_s)
        xs = _ffn(x1, p_sample[l].reshape(dbatch * dseq, -1), wu, wd, ln[2], ln[3], wpg, wp, alpha, tm_s)

    def stack(lst, lead, tail):
        return jnp.stack(lst).reshape(depth, *lead, *tail)

    lp, ls = (batch, seq), (dbatch, dseq)
    return (xp.reshape(batch, seq, D_MODEL), xs.reshape(dbatch, dseq, D_MODEL),
            stack(outs[0], lp, (KV_LORA,)), stack(outs[1], ls, (KV_LORA,)),
            stack(outs[2], lp, (DR,)), stack(outs[3], ls, (DR,)),
            stack(outs[4], lp, (H_B, 2 * DH)), stack(outs[5], ls, (H_B, 2 * DH)),
            stack(outs[6], lp, (H_B, DV_B)), stack(outs[7], ls, (H_B, DV_B)))
```

```python
import functools
import math

import jax
import jax.numpy as jnp
from jax import lax
from jax.experimental import pallas as pl
from jax.experimental.pallas import tpu as pltpu

F32 = jnp.float32
BF16 = jnp.bfloat16
I32 = jnp.int32

D_MODEL = 1024
H_A, DN, DR, DV_A, KV_LORA = 8, 64, 32, 64, 256
H_B, DH, DV_B = 8, 32, 64
MLA_SCALE = (DN + DR) ** -0.5
DIFF_SCALE = DH ** -0.5
ROPE_THETA = 10000.0
N_BUCKETS, MAX_DIST = 32, 128
D_FF = 4 * D_MODEL
DEPTH_ALPHA = lambda depth: (2 * depth) ** 0.25
EPS = 1e-5
NEG = -1e30

LANES = 128
HEAD_PAD = LANES
ROPE_LO = DN
VMEM_LIMIT = 56 * 1024 * 1024

QC0, QC1 = 0, H_A * HEAD_PAD
CKV0, CKV1 = QC1, QC1 + KV_LORA
KR0, KR1 = CKV1, CKV1 + HEAD_PAD
DQ0, DQ1 = KR1, KR1 + H_B * 2 * DH
DK0, DK1 = DQ1, DQ1 + H_B * 2 * DH
DV0, DV1 = DK1, DK1 + H_B * DV_B
N_ATT = DV1


def _dot(a, b):
    return jnp.dot(a, b, preferred_element_type=F32)


def _dot_nt(a, b):
    return lax.dot_general(a, b, (((1,), (1,)), ((), ())), preferred_element_type=F32)


def _const_spec(shape):
    nd = len(shape)
    return pl.BlockSpec(shape, lambda *_: (0,) * nd)


def _params(sem):
    return pltpu.CompilerParams(dimension_semantics=sem, vmem_limit_bytes=VMEM_LIMIT)


def _rope128(x, c, sa, sb):
    return x * c + pltpu.roll(x, LANES - DR // 2, 1) * sa + pltpu.roll(x, DR // 2, 1) * sb


def _store_head_padded(out_ref, z, fill):
    rows = z.shape[0]
    low = lax.broadcasted_iota(I32, (rows, LANES), 1) < DV_B
    for g in range(z.shape[1] // LANES):
        pair = z[:, g * LANES:(g + 1) * LANES]
        out_ref[:, (2 * g) * LANES:(2 * g + 1) * LANES] = jnp.where(low, pair, fill).astype(BF16)
        swapped = pltpu.roll(pair, LANES // 2, 1)
        out_ref[:, (2 * g + 1) * LANES:(2 * g + 2) * LANES] = jnp.where(low, swapped, fill).astype(BF16)


def _proj_body(x_ref, w_ref, c_ref, sa_ref, sb_ref, kvg_ref, wuk_ref, wuv_ref,
               qc_ref, dq_ref, ckv_ref, kr_ref, dk_ref, dv_ref, kc_ref, vm_ref, dqp_ref, dkp_ref, dvp_ref):
    xb = x_ref[...].astype(BF16)
    c, sa, sb = c_ref[...], sa_ref[...], sb_ref[...]
    zq = _dot(xb, w_ref[:, QC0:QC1])
    for h in range(H_A):
        sl = slice(h * HEAD_PAD, (h + 1) * HEAD_PAD)
        qc_ref[:, sl] = _rope128(zq[:, sl], c, sa, sb).astype(BF16)
    zc = _dot(xb, w_ref[:, CKV0:CKV1])
    ckv = zc * lax.rsqrt(jnp.mean(zc * zc, -1, keepdims=True) + EPS) * kvg_ref[...]
    ckv_ref[...] = ckv
    cb = ckv.astype(BF16)
    zk = _rope128(_dot(xb, w_ref[:, KR0:KR1]), c, sa, sb)
    kr_ref[...] = zk[:, ROPE_LO:ROPE_LO + DR]
    kn = _dot(cb, wuk_ref[...])
    for h in range(H_A):
        sl = slice(h * HEAD_PAD, (h + 1) * HEAD_PAD)
        kc_ref[:, sl] = (kn[:, sl] + zk).astype(BF16)
    _store_head_padded(vm_ref, _dot(cb, wuv_ref[...]), 1.0)
    zdq = _dot(xb, w_ref[:, DQ0:DQ1])
    dq_ref[...] = zdq.astype(BF16)
    _store_head_padded(dqp_ref, zdq, 0.0)
    zdk = _dot(xb, w_ref[:, DK0:DK1])
    dk_ref[...] = zdk
    _store_head_padded(dkp_ref, zdk, 0.0)
    zdv = _dot(xb, w_ref[:, DV0:DV1])
    dv_ref[...] = zdv
    _store_head_padded(dvp_ref, zdv, 1.0)


def _proj(x, w_att, tabs, kvg, wuk_pad, wuv2d, tm):
    t = x.shape[0]
    c, sa, sb = tabs
    tab_blocks = c.shape[0] // tm
    row = lambda w: pl.BlockSpec((tm, w), lambda i: (i, 0))
    tab = pl.BlockSpec((tm, LANES), lambda i: (i % tab_blocks, 0))
    out_shapes = [
        jax.ShapeDtypeStruct((t, H_A * HEAD_PAD), BF16),
        jax.ShapeDtypeStruct((t, H_B * 2 * DH), BF16),
        jax.ShapeDtypeStruct((t, KV_LORA), F32),
        jax.ShapeDtypeStruct((t, DR), F32),
        jax.ShapeDtypeStruct((t, H_B * 2 * DH), F32),
        jax.ShapeDtypeStruct((t, H_B * DV_B), F32),
        jax.ShapeDtypeStruct((t, H_A * HEAD_PAD), BF16),
        jax.ShapeDtypeStruct((t, H_A * HEAD_PAD), BF16),
        jax.ShapeDtypeStruct((t, H_B * HEAD_PAD), BF16),
        jax.ShapeDtypeStruct((t, H_B * HEAD_PAD), BF16),
        jax.ShapeDtypeStruct((t, H_B * HEAD_PAD), BF16),
    ]
    return pl.pallas_call(
        _proj_body,
        grid=(t // tm,),
        in_specs=[row(D_MODEL), _const_spec(w_att.shape), tab, tab, tab,
                  _const_spec(kvg.shape), _const_spec(wuk_pad.shape), _const_spec(wuv2d.shape)],
        out_specs=[row(s.shape[1]) for s in out_shapes],
        out_shape=out_shapes,
        compiler_params=_params(("arbitrary",)),
        name="proj",
    )(x, w_att, c, sa, sb, kvg, wuk_pad, wuv2d)


def _lam_value(lq1, lk1, lq2, lk2, lam_init):
    a = jnp.exp(jnp.sum(lq1[...] * lk1[...], -1, keepdims=True))
    b = jnp.exp(jnp.sum(lq2[...] * lk2[...], -1, keepdims=True))
    return a - b + lam_init


def _online_aug(m_ref, acc_ref, idx, s, v_aug):
    m_prev = m_ref[idx]
    m_new = jnp.maximum(m_prev, jnp.max(s, -1, keepdims=True))
    alpha = jnp.exp(m_prev - m_new)
    p = jnp.exp(s - jnp.concatenate([m_new] * (s.shape[1] // LANES), axis=1))
    acc_ref[idx] = alpha * acc_ref[idx] + _dot(p.astype(BF16), v_aug)
    m_ref[idx] = m_new


def _online(m_ref, l_ref, acc_ref, rows, arows, s, v, v_transposed=False):
    m_prev = m_ref[rows]
    m_new = jnp.maximum(m_prev, jnp.max(s, -1, keepdims=True))
    alpha = jnp.exp(m_prev - m_new)
    p = jnp.exp(s - m_new)
    l_ref[rows] = alpha * l_ref[rows] + jnp.sum(p, -1, keepdims=True)
    pv = _dot_nt(p.astype(BF16), v) if v_transposed else _dot(p.astype(BF16), v)
    acc_ref[arows] = alpha * acc_ref[arows] + pv
    m_ref[rows] = m_new


def _attn_body(table_ref, bkt_ref, qc_ref, dq_ref, kc_ref, vm_ref, dk_ref, dv_ref,
               lq1, lk1, lq2, lk2, g_ref, oa_ref, ob_ref,
               bias_ref, qd_ref, m_ref, acc_ref, *, lam_init, tq):
    b = pl.program_id(0)
    i = pl.program_id(1)

    @pl.when((b == 0) & (i == 0))
    def _build_bias():
        bias_ref[...] = jnp.zeros(bias_ref.shape, F32)

        def per_bucket(n, carry):
            for t in range(3):
                hit = bkt_ref[t] == n
                for h in range(H_B):
                    bias_ref[t, h] = jnp.where(hit, table_ref[n, h], bias_ref[t, h])
            return carry

        lax.fori_loop(0, N_BUCKETS, per_bucket, 0)
        causal = (lax.broadcasted_iota(I32, (tq, tq), 0) >= lax.broadcasted_iota(I32, (tq, tq), 1))
        for h in range(H_B):
            bias_ref[2, h] = jnp.where(causal, bias_ref[2, h], NEG)
        bias_ref[2, H_B] = jnp.where(causal, 0.0, NEG)

    m_ref[...] = jnp.full(m_ref.shape, NEG, F32)
    acc_ref[...] = jnp.zeros(acc_ref.shape, F32)

    lane = lax.broadcasted_iota(I32, (tq, HEAD_PAD), 1)
    for h in range(H_B):
        qh = dq_ref[:, h * HEAD_PAD:(h + 1) * HEAD_PAD]
        zero = jnp.zeros_like(qh)
        qd_ref[2 * h] = jnp.where(lane < DH, qh, zero)
        qd_ref[2 * h + 1] = jnp.where(lane >= DH, qh, zero)

    def step(j, carry):
        t = 2 - jnp.minimum(i - j, 2)
        ks = pl.ds(pl.multiple_of(j * tq, tq), tq)
        mask = bias_ref[t, H_B]
        for h in range(H_A):
            hs = slice(h * HEAD_PAD, (h + 1) * HEAD_PAD)
            s = _dot_nt(qc_ref[:, hs], kc_ref[ks, hs]) + mask
            _online_aug(m_ref, acc_ref, h, s, vm_ref[ks, hs])
        for h in range(H_B):
            hs = slice(h * HEAD_PAD, (h + 1) * HEAD_PAD)
            kh = dk_ref[ks, hs]
            vh = dv_ref[ks, hs]
            bias = bias_ref[t, h]
            for jm in range(2):
                s = _dot_nt(qd_ref[2 * h + jm], kh) + bias
                _online_aug(m_ref, acc_ref, H_A + 2 * h + jm, s, vh)
        return carry

    lax.fori_loop(0, i + 1, step, 0)

    low = lane < DV_B

    def normalised(idx):
        a = acc_ref[idx]
        return a * pltpu.roll(1.0 / a, LANES // 2, 1)

    def store_pairs(out_ref, heads):
        for g in range(len(heads) // 2):
            pair = jnp.where(low, heads[2 * g], pltpu.roll(heads[2 * g + 1], LANES // 2, 1))
            out_ref[:, g * LANES:(g + 1) * LANES] = pair.astype(out_ref.dtype)

    lam = _lam_value(lq1, lk1, lq2, lk2, lam_init)
    g = g_ref[...]
    store_pairs(oa_ref, [normalised(h) for h in range(H_A)])
    diff = []
    for h in range(H_B):
        o = jnp.where(low, normalised(H_A + 2 * h) - lam * normalised(H_A + 2 * h + 1), 0.0)
        ms = jnp.sum(o * o, -1, keepdims=True) * (1.0 / DV_B)
        diff.append(o * lax.rsqrt(ms + EPS) * g * (1.0 - lam_init))
    store_pairs(ob_ref, diff)


def _prompt_attention(table, bkt, qc, dq, kc, vm, dkb, dvb, lams, g, lam_init, batch, seq, tq):
    r3 = lambda a: a.reshape(batch, seq, a.shape[-1])
    qc, dq, kc, vm, dkb, dvb = map(r3, (qc, dq, kc, vm, dkb, dvb))
    qblk = lambda w: pl.BlockSpec((None, tq, w), lambda b, i: (b, i, 0))
    full = lambda w: pl.BlockSpec((None, seq, w), lambda b, i: (b, 0, 0), pipeline_mode=pl.Buffered(1))
    small = [_const_spec(a.shape) for a in lams] + [_const_spec(g.shape)]
    oa, ob = pl.pallas_call(
        functools.partial(_attn_body, lam_init=lam_init, tq=tq),
        grid=(batch, seq // tq),
        in_specs=[pl.BlockSpec(memory_space=pltpu.SMEM), _const_spec(bkt.shape),
                  qblk(qc.shape[-1]), qblk(dq.shape[-1]),
                  full(kc.shape[-1]), full(vm.shape[-1]), full(dkb.shape[-1]), full(dvb.shape[-1])] + small,
        out_specs=[qblk(H_A * DV_A), qblk(H_B * DV_B)],
        out_shape=[jax.ShapeDtypeStruct((batch, seq, H_A * DV_A), BF16),
                   jax.ShapeDtypeStruct((batch, seq, H_B * DV_B), BF16)],
        scratch_shapes=[pltpu.VMEM((3, H_B + 1, tq, tq), F32),
                        pltpu.VMEM((2 * H_B, tq, HEAD_PAD), BF16),
                        pltpu.VMEM((H_A + 2 * H_B, tq, LANES), F32),
                        pltpu.VMEM((H_A + 2 * H_B, tq, HEAD_PAD), F32)],
        compiler_params=_params(("arbitrary", "arbitrary")),
        name="prompt_attn",
    )(table, bkt, qc, dq, kc, vm, dkb, dvb, *lams, g)
    return oa.reshape(batch * seq, -1), ob.reshape(batch * seq, -1)


NEW_PAD = LANES
PAGE_SLOTS = 3


def _dec_body(pt_ref, qc_ref, dq_ref, ckvn_ref, krn_ref, dkn_ref, dvn_ref,
              wuk_ref, sel_ref, wuv_ref, tcol_ref, bktp_ref, bktn_ref,
              lq1, lk1, lq2, lk2, g_ref, lat_hbm, kr_hbm, k_hbm, v_hbm, oa_ref, ob_ref,
              bias_ref, biasn_ref, kb_ref, krt_ref, kt_ref, vt_ref, nb_ref, nk_ref, nv_ref,
              qbd_ref, qm_ref, qd_ref, m_ref, l_ref, acca_ref, accd_ref,
              lat_buf, kr_buf, k_buf, v_buf, sems, *, layer, lam_init, n_pg, page, n_q, n_seq, chunks):
    b = pl.program_id(0)
    c = pl.program_id(1)
    last = chunks - 1
    rows_a = n_q * H_A
    rows_d = 2 * n_q * H_B

    def page_copies(slot, seq, chunk):
        copies = []
        for p in range(n_pg):
            pid = pt_ref[seq, chunk * n_pg + p]
            cols = pl.ds(p * page, page)
            copies += [
                pltpu.make_async_copy(lat_hbm.at[layer, pid], lat_buf.at[slot, cols, :], sems.at[slot, 0]),
                pltpu.make_async_copy(kr_hbm.at[layer, pid], kr_buf.at[slot, :, cols], sems.at[slot, 1]),
                pltpu.make_async_copy(k_hbm.at[layer, pid], k_buf.at[slot, :, cols], sems.at[slot, 2]),
                pltpu.make_async_copy(v_hbm.at[layer, pid], v_buf.at[slot, :, cols], sems.at[slot, 3]),
            ]
        return copies

    def start_pages(slot, seq, chunk):
        for n, cp in enumerate(page_copies(slot, seq, chunk)):
            cp.start(priority=(n // 4) % 2)

    step = b * chunks + c
    total = n_seq * chunks
    ahead = PAGE_SLOTS - 1
    assert total >= ahead

    @pl.when(step == 0)
    def _prologue():
        for d in range(ahead):
            start_pages(d % PAGE_SLOTS, d // chunks, d % chunks)

    nxt = step + ahead

    @pl.when(nxt < total)
    def _prefetch():
        start_pages(lax.rem(nxt, PAGE_SLOTS), nxt // chunks, lax.rem(nxt, chunks))

    slot = lax.rem(step, PAGE_SLOTS)
    for cp in page_copies(slot, b, c):
        cp.wait()

    @pl.when((b == 0) & (c == 0))
    def _once():
        bias_ref[...] = jnp.zeros(bias_ref.shape, F32)
        biasn_ref[...] = jnp.zeros(biasn_ref.shape, F32)

        def per_bucket(n, carry):
            col = tcol_ref[n]
            for t in range(2):
                bias_ref[t] = jnp.where(bktp_ref[t] == n, col, bias_ref[t])
            biasn_ref[...] = jnp.where(bktn_ref[...] == n, col, biasn_ref[...])
            return carry

        lax.fori_loop(0, N_BUCKETS, per_bucket, 0)

    @pl.when(c == 0)
    def _per_sequence():
        m_ref[...] = jnp.full(m_ref.shape, NEG, F32)
        l_ref[...] = jnp.zeros(l_ref.shape, F32)
        acca_ref[...] = jnp.zeros(acca_ref.shape, F32)
        accd_ref[...] = jnp.zeros(accd_ref.shape, F32)
        head_a = lax.broadcasted_iota(I32, (H_A, H_A * HEAD_PAD), 1) // HEAD_PAD
        row_a = lax.broadcasted_iota(I32, (H_A, H_A * HEAD_PAD), 0)
        lane_d = lax.broadcasted_iota(I32, (H_B, H_B * 2 * DH), 1)
        row_d = lax.broadcasted_iota(I32, (H_B, H_B * 2 * DH), 0)
        qcf = qc_ref[...].astype(F32)
        dqf = dq_ref[...].astype(F32)
        for q in range(n_q):
            qrow = jnp.broadcast_to(qcf[q:q + 1, :], (H_A, H_A * HEAD_PAD))
            qbd_ref[q * H_A:(q + 1) * H_A, :] = jnp.where(head_a == row_a, qrow, 0.0).astype(BF16)
            drow = jnp.broadcast_to(dqf[q:q + 1, :], (H_B, H_B * 2 * DH))
            for jm in range(2):
                keep = (lane_d // (2 * DH) == row_d) & ((lane_d % (2 * DH)) // DH == jm)
                r0 = jm * n_q * H_B + q * H_B
                qd_ref[r0:r0 + H_B, :] = jnp.where(keep, drow, 0.0).astype(BF16)
        qbd = qbd_ref[...]
        qm_ref[:, 0:KV_LORA] = _dot_nt(qbd, wuk_ref[...]).astype(BF16)
        qm_ref[:, KV_LORA:] = _dot(qbd, sel_ref[...]).astype(BF16)

    kb_ref[...] = lat_buf[slot].astype(BF16)
    krt_ref[...] = kr_buf[slot].astype(BF16)
    kt_ref[...] = k_buf[slot].astype(BF16)
    vt_ref[...] = v_buf[slot].astype(BF16)

    ra = pl.ds(0, rows_a)
    rd = pl.ds(rows_a, rows_d)
    s = (_dot_nt(qm_ref[:, 0:KV_LORA], kb_ref[...])
         + _dot(qm_ref[:, KV_LORA:KV_LORA + DR], krt_ref[...]))
    _online(m_ref, l_ref, acca_ref, ra, slice(None), s, kb_ref[...])
    t = jnp.where(c == last, 1, 0)
    sd = _dot(qd_ref[...], kt_ref[...]) + bias_ref[t]
    _online(m_ref, l_ref, accd_ref, rd, slice(None), sd, vt_ref[...], v_transposed=True)

    @pl.when(c == last)
    def _finish():
        nb_ref[...] = jnp.zeros(nb_ref.shape, F32)
        nk_ref[...] = jnp.zeros(nk_ref.shape, F32)
        nv_ref[...] = jnp.zeros(nv_ref.shape, F32)
        nb_ref[0:n_q, 0:KV_LORA] = ckvn_ref[...]
        nb_ref[0:n_q, KV_LORA:KV_LORA + DR] = krn_ref[...]
        nk_ref[0:n_q, :] = dkn_ref[...]
        nv_ref[0:n_q, :] = dvn_ref[...]
        nb = nb_ref[...].astype(BF16)
        key = lax.broadcasted_iota(I32, (rows_a, NEW_PAD), 1)
        qry = lax.broadcasted_iota(I32, (rows_a, NEW_PAD), 0) // H_A
        s = jnp.where(key <= qry, _dot_nt(qm_ref[...], nb), NEG)
        _online(m_ref, l_ref, acca_ref, ra, slice(None), s, nb[:, 0:KV_LORA])
        key = lax.broadcasted_iota(I32, (rows_d, NEW_PAD), 1)
        qry = (lax.broadcasted_iota(I32, (rows_d, NEW_PAD), 0) % (n_q * H_B)) // H_B
        sd = _dot_nt(qd_ref[...], nk_ref[...].astype(BF16)) + biasn_ref[...]
        sd = jnp.where(key <= qry, sd, NEG)
        _online(m_ref, l_ref, accd_ref, rd, slice(None), sd, nv_ref[...].astype(BF16))

        o_lat = (acca_ref[...] / l_ref[ra]).astype(BF16)
        ra_full = _dot(o_lat, wuv_ref[...])
        own_a = (lax.broadcasted_iota(I32, (rows_a, H_A * DV_A), 1) // DV_A
                 == lax.broadcasted_iota(I32, (rows_a, H_A * DV_A), 0) % H_A)
        ra_full = jnp.where(own_a, ra_full, 0.0)
        lam = _lam_value(lq1, lk1, lq2, lk2, lam_init)
        half = n_q * H_B
        o1 = accd_ref[0:half, :] / l_ref[pl.ds(rows_a, half)]
        o2 = accd_ref[half:2 * half, :] / l_ref[pl.ds(rows_a + half, half)]
        own_d = (lax.broadcasted_iota(I32, (half, H_B * DV_B), 1) // DV_B
                 == lax.broadcasted_iota(I32, (half, H_B * DV_B), 0) % H_B)
        o = jnp.where(own_d, o1 - lam * o2, 0.0)
        ms = jnp.sum(o * o, -1, keepdims=True) * (1.0 / DV_B)
        o = o * lax.rsqrt(ms + EPS) * g_ref[...] * (1.0 - lam_init)
        for q in range(n_q):
            oa_ref[q:q + 1, :] = jnp.sum(ra_full[q * H_A:(q + 1) * H_A, :], 0, keepdims=True)
            ob_ref[q:q + 1, :] = jnp.sum(o[q * H_B:(q + 1) * H_B, :], 0, keepdims=True)


def _sample_attention(layer, page_table, caches, qc, dq, ckv, kr, dk, dv, wuk_pad, sel, wuv2d,
                      tcol, bktp, bktn, lams, g_tiled, lam_init, n_pg):
    lat_pool, kr_pool, k_pool, v_pool = caches
    bsz, n_pages = page_table.shape
    n_q = qc.shape[0] // bsz
    page = lat_pool.shape[2]
    assert page == LANES
    ck = n_pg * page
    r3 = lambda a: a.reshape(bsz, n_q, a.shape[-1])
    per_seq = [r3(a) for a in (qc, dq, ckv, kr, dk, dv)]
    seq_spec = lambda a: pl.BlockSpec((None, n_q, a.shape[-1]), lambda b, c, pt: (b, 0, 0))
    cst = lambda a: pl.BlockSpec(a.shape, lambda b, c, pt, nd=a.ndim: (0,) * nd)

    hbm = pl.BlockSpec(memory_space=pl.ANY)
    chunks = n_pages // n_pg
    consts = [wuk_pad, sel, wuv2d, tcol, bktp, bktn, *lams, g_tiled]
    rows_a, rows_d = n_q * H_A, 2 * n_q * H_B
    lat_w = KV_LORA + LANES
    kw = k_pool.shape[2]
    grid_spec = pltpu.PrefetchScalarGridSpec(
        num_scalar_prefetch=1,
        grid=(bsz, chunks),
        in_specs=[seq_spec(a) for a in per_seq] + [cst(a) for a in consts] + [hbm] * 4,
        out_specs=[pl.BlockSpec((None, n_q, H_A * DV_A), lambda b, c, pt: (b, 0, 0)),
                   pl.BlockSpec((None, n_q, H_B * DV_B), lambda b, c, pt: (b, 0, 0))],
        scratch_shapes=[pltpu.VMEM((2, rows_d, ck), F32),
                        pltpu.VMEM((rows_d, NEW_PAD), F32),
                        pltpu.VMEM((ck, KV_LORA), BF16),
                        pltpu.VMEM((DR, ck), BF16),
                        pltpu.VMEM((kw, ck), BF16),
                        pltpu.VMEM((kw, ck), BF16),
                        pltpu.VMEM((NEW_PAD, lat_w), F32),
                        pltpu.VMEM((NEW_PAD, kw), F32),
                        pltpu.VMEM((NEW_PAD, kw), F32),
                        pltpu.VMEM((rows_a, H_A * HEAD_PAD), BF16),
                        pltpu.VMEM((rows_a, lat_w), BF16),
                        pltpu.VMEM((rows_d, kw), BF16),
                        pltpu.VMEM((rows_a + rows_d, 1), F32),
                        pltpu.VMEM((rows_a + rows_d, 1), F32),
                        pltpu.VMEM((rows_a, KV_LORA), F32),
                        pltpu.VMEM((rows_d, kw), F32),
                        pltpu.VMEM((PAGE_SLOTS, ck, KV_LORA), F32),
                        pltpu.VMEM((PAGE_SLOTS, DR, ck), F32),
                        pltpu.VMEM((PAGE_SLOTS, kw, ck), F32),
                        pltpu.VMEM((PAGE_SLOTS, kw, ck), F32),
                        pltpu.SemaphoreType.DMA((PAGE_SLOTS, 4))],
    )
    oa, ob = pl.pallas_call(
        functools.partial(_dec_body, layer=layer, lam_init=lam_init, n_pg=n_pg, page=page, n_q=n_q,
                          n_seq=bsz, chunks=chunks),
        grid_spec=grid_spec,
        out_shape=[jax.ShapeDtypeStruct((bsz, n_q, H_A * DV_A), F32),
                   jax.ShapeDtypeStruct((bsz, n_q, H_B * DV_B), F32)],
        compiler_params=_params(("arbitrary", "arbitrary")),
        name="sample_attn",
    )(page_table, *per_seq, *consts, lat_pool, kr_pool, k_pool, v_pool)
    return oa.reshape(bsz * n_q, -1), ob.reshape(bsz * n_q, -1)


def _layer_norm(y, g, b):
    yc = y - jnp.mean(y, -1, keepdims=True)
    var = jnp.mean(yc * yc, -1, keepdims=True)
    return yc * lax.rsqrt(var + EPS) * g + b


def _merge_body(x_ref, oa_ref, ob_ref, wg_ref, wa_ref, wb_ref, wo_ref, g_ref, b_ref, y_ref, *, alpha):
    x = x_ref[...]
    xb = x.astype(BF16)
    ga = jax.nn.sigmoid(_dot(xb, wg_ref[:, 0:D_MODEL]))
    gb = jax.nn.sigmoid(_dot(xb, wg_ref[:, D_MODEL:2 * D_MODEL]))
    mix = (ga * _dot(oa_ref[...].astype(BF16), wa_ref[...])
           + gb * _dot(ob_ref[...].astype(BF16), wb_ref[...]))
    y = alpha * x + _dot(mix.astype(BF16), wo_ref[...])
    y_ref[...] = _layer_norm(y, g_ref[...], b_ref[...])


def _merge(x, oa, ob, wg, wa, wb, wo, g, b, alpha, tm):
    t = x.shape[0]
    row = lambda a: pl.BlockSpec((tm, a.shape[1]), lambda i: (i, 0))
    ws = [wg, wa, wb, wo, g, b]
    return pl.pallas_call(
        functools.partial(_merge_body, alpha=alpha),
        grid=(t // tm,),
        in_specs=[row(x), row(oa), row(ob)] + [_const_spec(w.shape) for w in ws],
        out_specs=row(x),
        out_shape=jax.ShapeDtypeStruct(x.shape, F32),
        compiler_params=_params(("arbitrary",)),
        name="merge",
    )(x, oa, ob, *ws)


FF_CHUNK = 512


def _ffn_body(x_ref, p_ref, wu_ref, wd_ref, g_ref, b_ref, wpg_ref, wp_ref, y_ref, *, alpha):
    x = x_ref[...]
    xb = x.astype(BF16)
    h = jnp.zeros(x.shape, F32)
    for k in range(D_FF // FF_CHUNK):
        sl = slice(k * FF_CHUNK, (k + 1) * FF_CHUNK)
        u = jnp.maximum(_dot(xb, wu_ref[:, sl]), 0.0)
        h = h + _dot((u * u).astype(BF16), wd_ref[sl, :])
    y = _layer_norm(alpha * x + h, g_ref[...], b_ref[...])
    gate = jax.nn.sigmoid(_dot(y.astype(BF16), wpg_ref[...]))
    y_ref[...] = y + gate * _dot(p_ref[...].astype(BF16), wp_ref[...])


def _ffn(x, p, wu, wd, g, b, wpg, wp, alpha, tm):
    t = x.shape[0]
    row = lambda a: pl.BlockSpec((tm, a.shape[1]), lambda i: (i, 0))
    ws = [wu, wd, g, b, wpg, wp]
    return pl.pallas_call(
        functools.partial(_ffn_body, alpha=alpha),
        grid=(t // tm,),
        in_specs=[row(x), row(p)] + [_const_spec(w.shape) for w in ws],
        out_specs=row(x),
        out_shape=jax.ShapeDtypeStruct(x.shape, F32),
        compiler_params=_params(("arbitrary",)),
        name="ffn",
    )(x, p, *ws)


def _t5_bucket(n):
    exact = N_BUCKETS // 2
    nf = jnp.maximum(n, 1).astype(F32)
    large = exact + (jnp.log(nf / exact) / math.log(MAX_DIST / exact) * (N_BUCKETS - exact)).astype(I32)
    large = jnp.minimum(large, N_BUCKETS - 1)
    return jnp.where(n < exact, n, large)


def _rope_tables(pos):
    half = DR // 2
    inv = ROPE_THETA ** (-jnp.arange(half, dtype=F32) / half)
    ang = pos.astype(F32)[:, None] * inv[None, :]
    cos, sin = jnp.cos(ang), jnp.sin(ang)
    n = pos.shape[0]
    ones = jnp.ones((n, ROPE_LO), F32)
    z = lambda w: jnp.zeros((n, w), F32)
    pad = HEAD_PAD - ROPE_LO - DR
    c = jnp.concatenate([ones, cos, cos, z(pad)], 1)
    sa = jnp.concatenate([z(ROPE_LO), -sin, z(half), z(pad)], 1)
    sb = jnp.concatenate([z(ROPE_LO), z(half), sin, z(pad)], 1)
    return c, sa, sb


def _layer_weights(w_in, w_uk, w_uv):
    d = w_in.shape[0]
    o = 0
    qn = w_in[:, o:o + H_A * DN].reshape(d, H_A, DN) * MLA_SCALE; o += H_A * DN
    qr = w_in[:, o:o + H_A * DR].reshape(d, H_A, DR) * MLA_SCALE; o += H_A * DR
    w_ckv = w_in[:, o:o + KV_LORA]; o += KV_LORA
    kr = w_in[:, o:o + DR]; o += DR
    w_dq = w_in[:, o:o + H_B * 2 * DH] * DIFF_SCALE; o += H_B * 2 * DH
    w_dk = w_in[:, o:o + H_B * 2 * DH]; o += H_B * 2 * DH
    w_dv = w_in[:, o:o + H_B * DV_B]; o += H_B * DV_B
    w_g = w_in[:, o:]
    pad = HEAD_PAD - DN - DR
    w_qc = jnp.concatenate([qn, qr, jnp.zeros((d, H_A, pad), F32)], -1).reshape(d, H_A * HEAD_PAD)
    w_kr = jnp.concatenate([jnp.zeros((d, ROPE_LO), F32), kr, jnp.zeros((d, pad), F32)], -1)
    w_att = jnp.concatenate([w_qc, w_ckv, w_kr, w_dq, w_dk, w_dv], 1).astype(BF16)
    wuk_pad = jnp.concatenate([w_uk, jnp.zeros((KV_LORA, H_A, HEAD_PAD - DN), F32)], -1)
    wuk_pad = wuk_pad.reshape(KV_LORA, H_A * HEAD_PAD).astype(BF16)
    wuv2d = w_uv.reshape(KV_LORA, H_A * DV_A).astype(BF16)
    return w_att, w_g.astype(BF16), wuk_pad, wuv2d


def kernel(x_prompt, x_sample, cache_mla_latent, cache_mla_krope, cache_diff_k, cache_diff_v,
           page_table, p_prompt, p_sample, rel_bias_table, w_in, kv_norm_g, w_uk, w_uv,
           lam_q1, lam_k1, lam_q2, lam_k2, subln_g, w_a, w_b, w_out, ln1_g, ln1_b,
           w_up, w_down, ln2_g, ln2_b, w_p, w_pg):
    batch, seq, _ = x_prompt.shape
    dbatch, dseq, _ = x_sample.shape
    depth = w_in.shape[0]
    n_pages = page_table.shape[1]
    page = cache_mla_latent.shape[2]
    past = n_pages * page
    alpha = DEPTH_ALPHA(depth)

    tq = 256
    tm_p = 512
    tm_s = min(256, dbatch * dseq)
    n_pg = 16
    ck = n_pg * page
    assert seq % tq == 0 and seq % tm_p == 0 and tm_s % dseq == 0 and n_pages % n_pg == 0
    assert tq > MAX_DIST and ck >= MAX_DIST

    tabs_p = _rope_tables(jnp.arange(seq))
    tabs_s = tuple(jnp.tile(t, (tm_s // dseq, 1)) for t in _rope_tables(past + jnp.arange(dseq)))
    qi = jnp.arange(tq)[:, None]
    ki = jnp.arange(tq)[None, :]
    bkt_p = jnp.stack([_t5_bucket(jnp.maximum(qi - ki + d * tq, 0)) for d in (2, 1, 0)]).astype(I32)
    rows_d = 2 * dseq * H_B
    q_of_row = (jnp.arange(rows_d) % (dseq * H_B)) // H_B
    h_of_row = jnp.arange(rows_d) % H_B
    kc_idx = jnp.arange(ck)[None, :]
    bkt_far = _t5_bucket(past + q_of_row[:, None] - kc_idx)
    bkt_last = _t5_bucket(ck + q_of_row[:, None] - kc_idx)
    bktp = jnp.stack([bkt_far, bkt_last]).astype(I32)
    bktn = _t5_bucket(jnp.maximum(q_of_row[:, None] - jnp.arange(NEW_PAD)[None, :], 0)).astype(I32)
    tcol = rel_bias_table[:, h_of_row][:, :, None]
    sel = jnp.zeros((H_A * HEAD_PAD, LANES), F32)
    rope_rows = (jnp.arange(H_A)[:, None] * HEAD_PAD + ROPE_LO + jnp.arange(DR)[None, :]).reshape(-1)
    sel = sel.at[rope_rows, jnp.tile(jnp.arange(DR), H_A)].set(1.0).astype(BF16)

    n_pool = cache_mla_latent.shape[1]
    caches = (cache_mla_latent, jnp.transpose(cache_mla_krope, (0, 1, 3, 2)),
              jnp.transpose(cache_diff_k, (0, 1, 3, 4, 2)).reshape(depth, n_pool, H_B * 2 * DH, page),
              jnp.transpose(cache_diff_v, (0, 1, 3, 4, 2)).reshape(depth, n_pool, H_B * DV_B, page))

    xp = x_prompt.reshape(batch * seq, D_MODEL)
    xs = x_sample.reshape(dbatch * dseq, D_MODEL)
    outs = [[] for _ in range(8)]
    row2 = lambda a: a.reshape(1, -1)
    for l in range(depth):
        lam_init = 0.8 - 0.6 * math.exp(-0.3 * l)
        w_att, w_g, wuk_pad, wuv2d = _layer_weights(w_in[l], w_uk[l], w_uv[l])
        kvg = row2(kv_norm_g[l])
        lams = [row2(a[l]) for a in (lam_q1, lam_k1, lam_q2, lam_k2)]
        g_head = row2(subln_g[l])
        g_tiled = jnp.tile(g_head, (1, H_B))
        g_pad = jnp.concatenate([g_head, jnp.zeros((1, HEAD_PAD - DV_B), F32)], 1)
        wa, wb, wo = w_a[l].astype(BF16), w_b[l].astype(BF16), w_out[l].astype(BF16)
        wu, wd = w_up[l].astype(BF16), w_down[l].astype(BF16)
        wpg, wp = w_pg[l].astype(BF16), w_p[l].astype(BF16)
        ln = [row2(a[l]) for a in (ln1_g, ln1_b, ln2_g, ln2_b)]

        qc, _, ckv, kr, dk, dv, kc, vm, dqp, dkp, dvp = _proj(xp, w_att, tabs_p, kvg, wuk_pad, wuv2d, tm_p)
        oa, ob = _prompt_attention(rel_bias_table, bkt_p, qc, dqp, kc, vm, dkp, dvp, lams, g_pad,
                                   lam_init, batch, seq, tq)
        for lst, a in zip(outs[0::2], (ckv, kr, dk, dv)):
            lst.append(a)
        x1 = _merge(xp, oa, ob, w_g, wa, wb, wo, ln[0], ln[1], alpha, tm_p)
        xp = _ffn(x1, p_prompt[l].reshape(batch * seq, -1), wu, wd, ln[2], ln[3], wpg, wp, alpha, tm_p)

        qc, dq, ckv, kr, dk, dv = _proj(xs, w_att, tabs_s, kvg, wuk_pad, wuv2d, tm_s)[:6]
        oa, ob = _sample_attention(l, page_table, caches, qc, dq, ckv, kr, dk, dv, wuk_pad, sel, wuv2d,
                                   tcol, bktp, bktn, lams, g_tiled, lam_init, n_pg)
        for lst, a in zip(outs[1::2], (ckv, kr, dk, dv)):
            lst.append(a)
        x1 = _merge(xs, oa, ob, w_g, wa, wb, wo, ln[0], ln[1], alpha, tm_s)
        xs = _ffn(x1, p_sample[l].reshape(dbatch * dseq, -1), wu, wd, ln[2], ln[3], wpg, wp, alpha, tm_s)

    def stack(lst, lead, tail):
        return jnp.stack(lst).reshape(depth, *lead, *tail)

    lp, ls = (batch, seq), (dbatch, dseq)
    return (xp.reshape(batch, seq, D_MODEL), xs.reshape(dbatch, dseq, D_MODEL),
            stack(outs[0], lp, (KV_LORA,)), stack(outs[1], ls, (KV_LORA,)),
            stack(outs[2], lp, (DR,)), stack(outs[3], ls, (DR,)),
            stack(outs[4], lp, (H_B, 2 * DH)), stack(outs[5], ls, (H_B, 2 * DH)),
            stack(outs[6], lp, (H_B, DV_B)), stack(outs[7], ls, (H_B, DV_B)))
```
